```python
import math
import jax, jax.numpy as jnp
from jax import lax
import numpy as np

D_MODEL = 2048
BATCH = 2
SEQ = 4096
DEPTH = 2
DEC_BATCH = 128
DEC_SEQ = 4
PAST_LEN = 16384
PAGE_SIZE = 128

HEAD_DIM = 64
FOX_HEADS = D_MODEL // 4 // HEAD_DIM
FOX_KV_HEADS = 2
FOX_GROUP = FOX_HEADS // FOX_KV_HEADS
FOX_FORGET_BIAS = 3.0
NSA_HEADS = D_MODEL // 4 // HEAD_DIM
NSA_KV_HEADS = 1
NSA_GROUP = NSA_HEADS // NSA_KV_HEADS
NSA_BLOCK = 64
NSA_TOPK = 16
NSA_WINDOW = 512
NSA_FORCED_SCORE = 1e3
MLA_V_DIM = 128
MLA_HEADS = D_MODEL // 2 // MLA_V_DIM
MLA_Q_RANK = 384
MLA_KV_RANK = 128
MLA_NOPE_DIM = 64
MLA_ROPE_DIM = 32
MLA_QK_DIM = MLA_NOPE_DIM + MLA_ROPE_DIM
ROPE_BASE = 10000.0
FOX_WIDTH = FOX_HEADS * HEAD_DIM
NSA_WIDTH = NSA_HEADS * HEAD_DIM
MLA_WIDTH = MLA_HEADS * MLA_V_DIM
MIX_WIDTH = FOX_WIDTH + NSA_WIDTH + MLA_WIDTH
REL_BUCKETS = 32
REL_MAX_DIST = 128
D_FF = 5632
CONV_W = 3
PLE_DIM = 256
Q_BLOCK = 128
EPS = 1e-6
NEG = -1e30
IN_SPLITS = (FOX_WIDTH, FOX_KV_HEADS * HEAD_DIM, FOX_KV_HEADS * HEAD_DIM, FOX_HEADS,
             NSA_WIDTH, 6 * NSA_KV_HEADS * HEAD_DIM, 3 * NSA_HEADS,
             MLA_Q_RANK, MLA_KV_RANK, MLA_ROPE_DIM)
N_IN = sum(IN_SPLITS)

kernel_name = 'hybrid_fox_nsa_mla_decoder_step'


def rmsnorm(x, g):
    xf = x.astype(jnp.float32)
    y = xf * lax.rsqrt(jnp.mean(xf * xf, axis=-1, keepdims=True) + EPS)
    return (y * g.astype(jnp.float32)).astype(x.dtype)


def masked_probs(s, mask):
    s = jnp.where(mask, s.astype(jnp.float32), NEG)
    p = jnp.where(mask, jnp.exp(s - jnp.max(s, axis=-1, keepdims=True)), 0.0)
    return p / jnp.maximum(jnp.sum(p, axis=-1, keepdims=True), 1e-30)


def rel_bucket(dist):
    exact = REL_BUCKETS // 2
    n = jnp.maximum(dist, 0)
    log_ratio = jnp.log(jnp.maximum(n, 1).astype(jnp.float32) / exact) / math.log(REL_MAX_DIST / exact)
    large = jnp.minimum(exact + (log_ratio * (REL_BUCKETS - exact)).astype(jnp.int32), REL_BUCKETS - 1)
    return jnp.where(n < exact, n, large)


def rope_cos_sin(pos):
    inv = ROPE_BASE ** (-jnp.arange(0, MLA_ROPE_DIM, 2, dtype=jnp.float32) / MLA_ROPE_DIM)
    ang = pos.astype(jnp.float32)[:, None] * inv
    return jnp.cos(ang), jnp.sin(ang)


def apply_rope(x, cos, sin):
    x1, x2 = jnp.split(x.astype(jnp.float32), 2, axis=-1)
    return jnp.concatenate([x1 * cos - x2 * sin, x1 * sin + x2 * cos], axis=-1).astype(x.dtype)


def to_blocks(x):
    s_len = x.shape[0]
    nb = -(-s_len // NSA_BLOCK)
    x = jnp.pad(x, ((0, nb * NSA_BLOCK - s_len), (0, 0), (0, 0)))
    return x.reshape(nb, NSA_BLOCK, x.shape[1], x.shape[2]).transpose(2, 0, 1, 3)


def causal_dwconv(a_pad, w, b):
    out = lax.conv_general_dilated(a_pad, w[:, None, :].astype(a_pad.dtype), window_strides=(1,),
                                   padding='VALID', dimension_numbers=('NWC', 'WIO', 'NWC'),
                                   feature_group_count=a_pad.shape[-1])
    return out + b


def fox_core(q, k, v, cq, ck, qpos, kpos):
    t, s_len = q.shape[0], k.shape[0]
    qg = q.reshape(t, FOX_KV_HEADS, FOX_GROUP, HEAD_DIM)
    logits = jnp.einsum('tgrd,sgd->grts', qg, k).astype(jnp.float32) * HEAD_DIM ** -0.5
    decay = (cq.T.reshape(FOX_KV_HEADS, FOX_GROUP, t)[..., :, None]
             - ck.T.reshape(FOX_KV_HEADS, FOX_GROUP, s_len)[..., None, :])
    p = masked_probs(logits + decay, kpos[None, :] <= qpos[:, None])
    o = jnp.einsum('grts,sgd->tgrd', p.astype(v.dtype), v)
    return o.reshape(t, FOX_WIDTH)


def fox_prompt(q, k, v, logf):
    s_len = q.shape[0]
    n_qb = s_len // Q_BLOCK
    c = jnp.cumsum(logf.astype(jnp.float32), axis=0)
    pos = jnp.arange(s_len, dtype=jnp.int32)

    def block(args):
        qb, cb, pb = args
        return fox_core(qb, k, v, cb, c, pb, pos)

    out = lax.map(block, (q.reshape(n_qb, Q_BLOCK, FOX_HEADS, HEAD_DIM),
                          c.reshape(n_qb, Q_BLOCK, FOX_HEADS), pos.reshape(n_qb, Q_BLOCK)))
    return out.reshape(s_len, FOX_WIDTH)


def fox_sample(q, k_new, v_new, logf_new, kv_past, logf_past, qpos):
    p_len = kv_past.shape[0]
    k = jnp.concatenate([kv_past[:, 0], k_new], axis=0)
    v = jnp.concatenate([kv_past[:, 1], v_new], axis=0)
    lp = logf_past.astype(jnp.float32)
    c_past = lp - lax.cumsum(lp, axis=0, reverse=True)
    c_new = jnp.cumsum(logf_new.astype(jnp.float32), axis=0)
    c = jnp.concatenate([c_past, c_new], axis=0)
    kpos = jnp.concatenate([jnp.arange(p_len, dtype=jnp.int32), qpos])
    return fox_core(q, k, v, c_new, c, qpos, kpos)


def nsa_compress(kc, vc, phi, cpos, kn_c):
    kb, vb = to_blocks(kc), to_blocks(vc)
    k_c = jnp.einsum('gbjd,jde->bge', kb + cpos[0], phi[0])
    v_c = jnp.einsum('gbjd,jde->bge', vb + cpos[1], phi[1])
    cend = jnp.arange(kb.shape[1], dtype=jnp.int32) * NSA_BLOCK + (NSA_BLOCK - 1)
    return rmsnorm(k_c, kn_c), v_c, cend


def nsa_attend(q, gates, qpos, k_c, v_c, cend, ks_b, vs_b, kw, vw, kwpos, rel_tbl):
    t, nb = q.shape[0], k_c.shape[0]
    qg = q.reshape(t, NSA_KV_HEADS, NSA_GROUP, HEAD_DIM)
    scale = HEAD_DIM ** -0.5
    tbl = rel_tbl.reshape(REL_BUCKETS, NSA_KV_HEADS, NSA_GROUP).astype(jnp.float32)
    dist_c = qpos[:, None] - cend[None, :]
    s_c = (jnp.einsum('tgrd,bgd->grtb', qg, k_c).astype(jnp.float32) * scale
           + tbl[rel_bucket(dist_c)].transpose(2, 3, 0, 1))
    p_c = masked_probs(s_c, dist_c >= 0)
    o_c = jnp.einsum('grtb,bgd->tgrd', p_c.astype(v_c.dtype), v_c)
    bq = qpos // NSA_BLOCK
    bidx = jnp.arange(nb, dtype=jnp.int32)
    forced = (bidx[None, :] == 0) | (bidx[None, :] == bq[:, None]) | (bidx[None, :] == bq[:, None] - 1)
    imp = jnp.sum(p_c, axis=1).transpose(1, 0, 2)
    imp = jnp.where(forced[:, None, :], NSA_FORCED_SCORE, imp)
    imp = jnp.where((bidx[None, :] <= bq[:, None])[:, None, :], imp, NEG)
    n_sel = min(NSA_TOPK, nb)
    _, sel = lax.top_k(imp, n_sel)
    g_idx = jnp.arange(NSA_KV_HEADS)
    k_s = ks_b[g_idx[None, :, None], sel]
    v_s = vs_b[g_idx[None, :, None], sel]
    spos = sel[..., None] * NSA_BLOCK + jnp.arange(NSA_BLOCK, dtype=jnp.int32)
    dist_s = qpos[:, None, None, None] - spos
    b_s = tbl[rel_bucket(dist_s), g_idx[None, :, None, None]]
    s_s = (jnp.einsum('tgrd,tgkjd->grtkj', qg, k_s).astype(jnp.float32) * scale
           + b_s.transpose(1, 4, 0, 2, 3))
    n_keys = n_sel * NSA_BLOCK
    mask_s = (dist_s >= 0).transpose(1, 0, 2, 3).reshape(NSA_KV_HEADS, 1, t, n_keys)
    p_s = masked_probs(s_s.reshape(NSA_KV_HEADS, NSA_GROUP, t, n_keys), mask_s)
    o_s = jnp.einsum('grtn,tgnd->tgrd', p_s.astype(v_s.dtype),
                     v_s.transpose(0, 1, 2, 3, 4).reshape(t, NSA_KV_HEADS, n_keys, HEAD_DIM))
    dist_w = qpos[:, None] - kwpos[None, :]
    mask_w = (dist_w >= 0) & (dist_w <= NSA_WINDOW) & (kwpos[None, :] >= 0)
    s_w = (jnp.einsum('tgrd,sgd->grts', qg, kw).astype(jnp.float32) * scale
           + tbl[rel_bucket(dist_w)].transpose(2, 3, 0, 1))
    p_w = masked_probs(s_w, mask_w)
    o_w = jnp.einsum('grts,sgd->tgrd', p_w.astype(vw.dtype), vw)
    g = jax.nn.sigmoid(gates.astype(jnp.float32)).reshape(t, 3, NSA_KV_HEADS, NSA_GROUP)[..., None]
    o = g[:, 0] * o_c + g[:, 1] * o_s + g[:, 2] * o_w
    return o.reshape(t, NSA_WIDTH).astype(q.dtype)


def nsa_prompt(q, gates, rows, win, phi, cpos, kn, rel_tbl):
    s_len = q.shape[0]
    n_qb = s_len // Q_BLOCK
    k_c, v_c, cend = nsa_compress(rows[:, 0], rows[:, 1], phi, cpos, kn[0])
    ks_b, vs_b = to_blocks(rows[:, 2]), to_blocks(rows[:, 3])
    win_pad = jnp.pad(win, ((NSA_WINDOW, 0), (0, 0), (0, 0), (0, 0)))
    pos = jnp.arange(s_len, dtype=jnp.int32)
    span = NSA_WINDOW + Q_BLOCK

    def block(args):
        qb, gb, pb = args
        wb = lax.dynamic_slice_in_dim(win_pad, pb[0], span, axis=0)
        wpos = pb[0] - NSA_WINDOW + jnp.arange(span, dtype=jnp.int32)
        return nsa_attend(qb, gb, pb, k_c, v_c, cend, ks_b, vs_b, wb[:, 0], wb[:, 1], wpos, rel_tbl)

    out = lax.map(block, (q.reshape(n_qb, Q_BLOCK, NSA_HEADS, HEAD_DIM),
                          gates.reshape(n_qb, Q_BLOCK, 3, NSA_HEADS), pos.reshape(n_qb, Q_BLOCK)))
    return out.reshape(s_len, NSA_WIDTH)


def nsa_sample(q, gates, rows_new, win_new, rows_past, win_state, qpos, phi, cpos, kn, rel_tbl):
    rows = jnp.concatenate([rows_past, rows_new], axis=0)
    k_c, v_c, cend = nsa_compress(rows[:, 0], rows[:, 1], phi, cpos, kn[0])
    ks_b, vs_b = to_blocks(rows[:, 2]), to_blocks(rows[:, 3])
    win = jnp.concatenate([win_state, win_new], axis=0)
    p_len, wb = rows_past.shape[0], win_state.shape[0]
    wpos = jnp.concatenate([p_len - wb + jnp.arange(wb, dtype=jnp.int32), qpos])
    return nsa_attend(q, gates, qpos, k_c, v_c, cend, ks_b, vs_b, win[:, 0], win[:, 1], wpos, rel_tbl)


def mla_expand(ckv, wukv, kn):
    kv = (ckv @ wukv).reshape(ckv.shape[0], MLA_HEADS, MLA_NOPE_DIM + MLA_V_DIM)
    return rmsnorm(kv[..., :MLA_NOPE_DIM], kn[:MLA_NOPE_DIM]), kv[..., MLA_NOPE_DIM:]


def mla_core(q_nope, q_rope, k_nope, k_rope, v, qpos, kpos):
    logits = (jnp.einsum('thd,shd->hts', q_nope, k_nope)
              + jnp.einsum('thd,sd->hts', q_rope, k_rope)).astype(jnp.float32) * MLA_QK_DIM ** -0.5
    p = masked_probs(logits, kpos[None, :] <= qpos[:, None])
    return jnp.einsum('hts,shd->thd', p.astype(v.dtype), v).reshape(q_nope.shape[0], MLA_WIDTH)


def mla_prompt(q_nope, q_rope, lat, wukv, kn):
    s_len = q_nope.shape[0]
    n_qb = s_len // Q_BLOCK
    k_nope, v = mla_expand(lat[:, :MLA_KV_RANK], wukv, kn)
    k_rope = lat[:, MLA_KV_RANK:]
    pos = jnp.arange(s_len, dtype=jnp.int32)

    def block(args):
        qn, qr, pb = args
        return mla_core(qn, qr, k_nope, k_rope, v, pb, pos)

    out = lax.map(block, (q_nope.reshape(n_qb, Q_BLOCK, MLA_HEADS, MLA_NOPE_DIM),
                          q_rope.reshape(n_qb, Q_BLOCK, MLA_HEADS, MLA_ROPE_DIM), pos.reshape(n_qb, Q_BLOCK)))
    return out.reshape(s_len, MLA_WIDTH)


def mla_sample(q_nope, q_rope, lat_new, lat_past, qpos, wukv, kn):
    lat = jnp.concatenate([lat_past, lat_new], axis=0)
    k_nope, v = mla_expand(lat[:, :MLA_KV_RANK], wukv, kn)
    kpos = jnp.concatenate([jnp.arange(lat_past.shape[0], dtype=jnp.int32), qpos])
    return mla_core(q_nope, q_rope, k_nope, lat[:, MLA_KV_RANK:], v, qpos, kpos)


def project_tokens(x, pos, lw):
    n, t, _ = x.shape
    z = rmsnorm(x, lw['norm_mix']) @ lw['w_in']
    offsets = np.cumsum(IN_SPLITS)[:-1].tolist()
    fq, fk, fv, ff, nq, nkv, ng, mcq, mckv, mkr = jnp.split(z, offsets, axis=-1)
    kv = nkv.reshape(n, t, 6, NSA_KV_HEADS, HEAD_DIM)
    cos, sin = rope_cos_sin(pos)
    q_mla = (rmsnorm(mcq, lw['mla_cqn']) @ lw['mla_wuq']).reshape(n, t, MLA_HEADS, MLA_QK_DIM)
    return {
        'fox_q': rmsnorm(fq.reshape(n, t, FOX_HEADS, HEAD_DIM), lw['fox_qn']),
        'fox_k': rmsnorm(fk.reshape(n, t, FOX_KV_HEADS, HEAD_DIM), lw['fox_kn']),
        'fox_v': fv.reshape(n, t, FOX_KV_HEADS, HEAD_DIM),
        'fox_logf': jax.nn.log_sigmoid(ff.astype(jnp.float32) + lw['fox_fb'].astype(jnp.float32)),
        'nsa_q': rmsnorm(nq.reshape(n, t, NSA_HEADS, HEAD_DIM), lw['nsa_qn']),
        'nsa_g': ng.reshape(n, t, 3, NSA_HEADS),
        'nsa_rows': jnp.stack([kv[:, :, 0], kv[:, :, 1], rmsnorm(kv[:, :, 2], lw['nsa_kn'][1]), kv[:, :, 3]], axis=2),
        'win_rows': jnp.stack([rmsnorm(kv[:, :, 4], lw['nsa_kn'][2]), kv[:, :, 5]], axis=2),
        'mla_qn': rmsnorm(q_mla[..., :MLA_NOPE_DIM], lw['mla_qn'][:MLA_NOPE_DIM]),
        'mla_qr': apply_rope(rmsnorm(q_mla[..., MLA_NOPE_DIM:], lw['mla_qn'][MLA_NOPE_DIM:]), cos[:, None], sin[:, None]),
        'mla_rows': jnp.concatenate([rmsnorm(mckv, lw['mla_ckvn']),
                                     apply_rope(rmsnorm(mkr, lw['mla_kn'][MLA_NOPE_DIM:]), cos, sin)], axis=-1),
    }


def mix_prompt(tok, lw, rel_tbl):
    fox = jax.vmap(fox_prompt)(tok['fox_q'], tok['fox_k'], tok['fox_v'], tok['fox_logf'])
    nsa = jax.vmap(lambda q, g, r, w: nsa_prompt(q, g, r, w, lw['nsa_phi'], lw['nsa_cpos'], lw['nsa_kn'], rel_tbl))(
        tok['nsa_q'], tok['nsa_g'], tok['nsa_rows'], tok['win_rows'])
    mla = jax.vmap(lambda qn, qr, lat: mla_prompt(qn, qr, lat, lw['mla_wukv'], lw['mla_kn']))(
        tok['mla_qn'], tok['mla_qr'], tok['mla_rows'])
    return jnp.concatenate([fox, nsa, mla], axis=-1)


def mix_sample(tok, lw, rel_tbl, l, qpos, page_table, cache_fox_kv, cache_fox_logf, cache_nsa_kv, cache_mla, win_state):
    def gather(cache, pt):
        c = cache[l, pt]
        return c.reshape((c.shape[0] * c.shape[1],) + c.shape[2:])

    def per_seq(args):
        pt, fq, fk, fv, fl, nq, ng, nr, wr, ws, mqn, mqr, mr = args
        fox = fox_sample(fq, fk, fv, fl, gather(cache_fox_kv, pt), gather(cache_fox_logf, pt), qpos)
        nsa = nsa_sample(nq, ng, nr, wr, gather(cache_nsa_kv, pt), ws, qpos,
                         lw['nsa_phi'], lw['nsa_cpos'], lw['nsa_kn'], rel_tbl)
        mla = mla_sample(mqn, mqr, mr, gather(cache_mla, pt), qpos, lw['mla_wukv'], lw['mla_kn'])
        return jnp.concatenate([fox, nsa, mla], axis=-1)

    return lax.map(per_seq, (page_table, tok['fox_q'], tok['fox_k'], tok['fox_v'], tok['fox_logf'],
                             tok['nsa_q'], tok['nsa_g'], tok['nsa_rows'], tok['win_rows'], win_state,
                             tok['mla_qn'], tok['mla_qr'], tok['mla_rows']))


def channel_and_ple(h, mix, p, conv_state, lw):
    h = h + mix @ lw['w_out']
    a, b = jnp.split(rmsnorm(h, lw['norm_ffn']) @ lw['ffn_up'], 2, axis=-1)
    a_pad = jnp.concatenate([conv_state.astype(a.dtype), a], axis=1)
    c = causal_dwconv(a_pad, lw['ffn_conv'], lw['ffn_conv_b'])
    h = h + (jax.nn.silu(c) * b) @ lw['ffn_down']
    gate = jax.nn.sigmoid(rmsnorm(h, lw['ple_norm']) @ lw['ple_gate'])
    h = h + gate * (p @ lw['ple_proj'])
    return h, a_pad[:, -(CONV_W - 1):]


def setup_inputs(seed: int = 0) -> dict:
    key = jax.random.key(seed)
    keys = iter(jax.random.split(key, 48))

    def normal(shape, scale=1.0):
        return scale * jax.random.normal(next(keys), shape, jnp.float32)

    def gain(shape):
        return 1.0 + 0.05 * normal(shape)

    n_pages = PAST_LEN // PAGE_SIZE
    n_used = DEC_BATCH * n_pages
    n_pool = n_used + n_used // 4
    win_buf = min(NSA_WINDOW, PAST_LEN)
    lat = MLA_KV_RANK + MLA_ROPE_DIM
    page_table = jax.random.permutation(next(keys), n_pool)[:n_used].reshape(DEC_BATCH, n_pages).astype(jnp.int32)
    return {
        'x_prompt': normal((BATCH, SEQ, D_MODEL)),
        'x_sample': normal((DEC_BATCH, DEC_SEQ, D_MODEL)),
        'cache_fox_kv': normal((DEPTH, n_pool, PAGE_SIZE, 2, FOX_KV_HEADS, HEAD_DIM)),
        'cache_fox_logf': jax.nn.log_sigmoid(FOX_FORGET_BIAS + normal((DEPTH, n_pool, PAGE_SIZE, FOX_HEADS))),
        'cache_nsa_kv': normal((DEPTH, n_pool, PAGE_SIZE, 4, NSA_KV_HEADS, HEAD_DIM)),
        'cache_mla': normal((DEPTH, n_pool, PAGE_SIZE, lat)),
        'state_nsa_win': normal((DEPTH, DEC_BATCH, win_buf, 2, NSA_KV_HEADS, HEAD_DIM)),
        'state_ffn_conv': normal((DEPTH, DEC_BATCH, CONV_W - 1, D_FF)),
        'page_table': page_table,
        'p_prompt': normal((DEPTH, BATCH, SEQ, PLE_DIM)),
        'p_sample': normal((DEPTH, DEC_BATCH, DEC_SEQ, PLE_DIM)),
        'rel_bias': normal((REL_BUCKETS, NSA_HEADS), 0.5),
        'norm_mix': gain((DEPTH, D_MODEL)),
        'w_in': normal((DEPTH, D_MODEL, N_IN), D_MODEL ** -0.5),
        'fox_qn': gain((DEPTH, HEAD_DIM)),
        'fox_kn': gain((DEPTH, HEAD_DIM)),
        'fox_fb': FOX_FORGET_BIAS + normal((DEPTH, FOX_HEADS), 0.5),
        'nsa_qn': gain((DEPTH, HEAD_DIM)),
        'nsa_kn': gain((DEPTH, 3, HEAD_DIM)),
        'nsa_phi': normal((DEPTH, 2, NSA_BLOCK, HEAD_DIM, HEAD_DIM), (NSA_BLOCK * HEAD_DIM) ** -0.5),
        'nsa_cpos': normal((DEPTH, 2, NSA_BLOCK, HEAD_DIM), 0.5),
        'mla_cqn': gain((DEPTH, MLA_Q_RANK)),
        'mla_wuq': normal((DEPTH, MLA_Q_RANK, MLA_HEADS * MLA_QK_DIM), MLA_Q_RANK ** -0.5),
        'mla_qn': gain((DEPTH, MLA_QK_DIM)),
        'mla_ckvn': gain((DEPTH, MLA_KV_RANK)),
        'mla_wukv': normal((DEPTH, MLA_KV_RANK, MLA_HEADS * (MLA_NOPE_DIM + MLA_V_DIM)), MLA_KV_RANK ** -0.5),
        'mla_kn': gain((DEPTH, MLA_QK_DIM)),
        'w_out': normal((DEPTH, MIX_WIDTH, D_MODEL), MIX_WIDTH ** -0.5),
        'norm_ffn': gain((DEPTH, D_MODEL)),
        'ffn_up': normal((DEPTH, D_MODEL, 2 * D_FF), D_MODEL ** -0.5),
        'ffn_conv': normal((DEPTH, CONV_W, D_FF), CONV_W ** -0.5),
        'ffn_conv_b': normal((DEPTH, D_FF), 0.02),
        'ffn_down': normal((DEPTH, D_FF, D_MODEL), D_FF ** -0.5),
        'ple_norm': gain((DEPTH, D_MODEL)),
        'ple_gate': normal((DEPTH, D_MODEL, D_MODEL), D_MODEL ** -0.5),
        'ple_proj': normal((DEPTH, PLE_DIM, D_MODEL), PLE_DIM ** -0.5),
    }


def reference(x_prompt, x_sample, cache_fox_kv, cache_fox_logf, cache_nsa_kv, cache_mla,
              state_nsa_win, state_ffn_conv, page_table, p_prompt, p_sample,
              rel_bias, norm_mix, w_in, fox_qn, fox_kn, fox_fb, nsa_qn, nsa_kn, nsa_phi, nsa_cpos,
              mla_cqn, mla_wuq, mla_qn, mla_ckvn, mla_wukv, mla_kn, w_out, norm_ffn, ffn_up,
              ffn_conv, ffn_conv_b, ffn_down, ple_norm, ple_gate, ple_proj):
    past_len = page_table.shape[1] * cache_fox_kv.shape[2]
    pos_p = jnp.arange(x_prompt.shape[1], dtype=jnp.int32)
    pos_s = past_len + jnp.arange(x_sample.shape[1], dtype=jnp.int32)
    win_keep = min(NSA_WINDOW, x_prompt.shape[1])
    win_buf = state_nsa_win.shape[2]
    hp, hs = x_prompt, x_sample
    fkv_p, flf_p, nkv_p, mla_p, win_p, conv_p = [], [], [], [], [], []
    fkv_s, flf_s, nkv_s, mla_s, win_s, conv_s = [], [], [], [], [], []
    for l in range(DEPTH):
        lw = {'norm_mix': norm_mix[l], 'w_in': w_in[l], 'fox_qn': fox_qn[l], 'fox_kn': fox_kn[l],
              'fox_fb': fox_fb[l], 'nsa_qn': nsa_qn[l], 'nsa_kn': nsa_kn[l], 'nsa_phi': nsa_phi[l],
              'nsa_cpos': nsa_cpos[l], 'mla_cqn': mla_cqn[l], 'mla_wuq': mla_wuq[l], 'mla_qn': mla_qn[l],
              'mla_ckvn': mla_ckvn[l], 'mla_wukv': mla_wukv[l], 'mla_kn': mla_kn[l], 'w_out': w_out[l],
              'norm_ffn': norm_ffn[l], 'ffn_up': ffn_up[l], 'ffn_conv': ffn_conv[l],
              'ffn_conv_b': ffn_conv_b[l], 'ffn_down': ffn_down[l], 'ple_norm': ple_norm[l],
              'ple_gate': ple_gate[l], 'ple_proj': ple_proj[l]}
        tok = project_tokens(hp, pos_p, lw)
        mix = mix_prompt(tok, lw, rel_bias)
        hp, cst = channel_and_ple(hp, mix, p_prompt[l], jnp.zeros((hp.shape[0], CONV_W - 1, D_FF), hp.dtype), lw)
        fkv_p.append(jnp.stack([tok['fox_k'], tok['fox_v']], axis=2))
        flf_p.append(tok['fox_logf'])
        nkv_p.append(tok['nsa_rows'])
        mla_p.append(tok['mla_rows'])
        win_p.append(tok['win_rows'][:, -win_keep:])
        conv_p.append(cst)
        tok = project_tokens(hs, pos_s, lw)
        mix = mix_sample(tok, lw, rel_bias, l, pos_s, page_table, cache_fox_kv, cache_fox_logf,
                         cache_nsa_kv, cache_mla, state_nsa_win[l])
        hs, cst = channel_and_ple(hs, mix, p_sample[l], state_ffn_conv[l], lw)
        fkv_s.append(jnp.stack([tok['fox_k'], tok['fox_v']], axis=2))
        flf_s.append(tok['fox_logf'])
        nkv_s.append(tok['nsa_rows'])
        mla_s.append(tok['mla_rows'])
        win_s.append(jnp.concatenate([state_nsa_win[l], tok['win_rows']], axis=1)[:, -win_buf:])
        conv_s.append(cst)
    return (hp, hs,
            jnp.stack(fkv_p), jnp.stack(flf_p), jnp.stack(nkv_p), jnp.stack(mla_p), jnp.stack(win_p), jnp.stack(conv_p),
            jnp.stack(fkv_s), jnp.stack(flf_s), jnp.stack(nkv_s), jnp.stack(mla_s), jnp.stack(win_s), jnp.stack(conv_s))
```

```python
import functools
import math

import numpy as np
import jax
import jax.numpy as jnp
from jax import lax
from jax.experimental import pallas as pl
from jax.experimental.pallas import tpu as pltpu

F32, BF16, I32 = jnp.float32, jnp.bfloat16, jnp.int32

D_MODEL = 2048
HEAD_DIM = 64
N_HEADS = 8
FOX_KV_HEADS = 2
FOX_GROUP = N_HEADS // FOX_KV_HEADS
NSA_BLOCK = 64
NSA_TOPK = 16
NSA_WINDOW = 512
NSA_FORCED_SCORE = 1e3
MLA_V_DIM = 128
MLA_Q_RANK = 384
MLA_KV_RANK = 128
MLA_NOPE_DIM = 64
MLA_ROPE_DIM = 32
MLA_QK_DIM = MLA_NOPE_DIM + MLA_ROPE_DIM
ROPE_BASE = 10000.0
REL_BUCKETS = 32
REL_MAX_DIST = 128
D_FF = 5632
CONV_W = 3
EPS = 1e-6
NEG = -1e30

LANES = 128
SUBLANES = 8
VMEM_LIMIT = 56 * 1024 * 1024

Q_TILE = 128
ROWS_W = 7 * LANES
R_FOX, R_NSA, R_WIN, R_LAT, R_G = 0, 256, 512, 640, 768
G_KR, G_LOGF, G_GATE = 0, 32, 40
Z_FQ, Z_NQ, Z_FK, Z_FV, Z_NKV, Z_CKV, Z_CQ, Z_G, Z_END = 0, 1024, 1536, 1664, 1792, 2176, 2304, 2688, 2816


def _cparams(sem, vmem=VMEM_LIMIT):
    return pltpu.CompilerParams(dimension_semantics=sem, vmem_limit_bytes=vmem)


def _dot(a, b):
    return jnp.dot(a, b, preferred_element_type=F32)


def _dot_nt(a, b):
    return lax.dot_general(a, b, (((1,), (1,)), ((), ())), preferred_element_type=F32)


def _seg_mean(x, seg):
    hi = x.astype(BF16)
    lo = (x - hi.astype(F32)).astype(BF16)
    return _dot(hi, seg) + _dot(lo, seg)


def _lane(shape):
    return lax.broadcasted_iota(I32, shape, len(shape) - 1)


def _proj_kernel(x_ref, gmix_ref, w_ref, gain_ref, seg_ref, wuq_ref, gmla_ref, wuk_ref, rope_ref,
                 qf_ref, qn_ref, qm_ref, rows_ref):
    x = x_ref[...]
    xn = (x * lax.rsqrt(jnp.mean(x * x, axis=-1, keepdims=True) + EPS) * gmix_ref[...]).astype(BF16)
    z = _dot(xn, w_ref[...])
    gain = gain_ref[...]
    seg_pair, seg_first, seg_kr, seg_mla = seg_ref[0], seg_ref[1], seg_ref[2], seg_ref[3]
    lane = _lane((x.shape[0], LANES))

    def head_norm(lo, seg):
        zs = z[:, lo:lo + LANES]
        return zs * lax.rsqrt(_seg_mean(zs * zs, seg) + EPS) * gain[:, lo:lo + LANES]

    for i in range(8):
        qf_ref[:, i * LANES:(i + 1) * LANES] = head_norm(Z_FQ + i * LANES, seg_pair).astype(BF16)
    for i in range(4):
        qn_ref[:, i * LANES:(i + 1) * LANES] = head_norm(Z_NQ + i * LANES, seg_pair).astype(BF16)
    rows_ref[:, R_FOX:R_FOX + LANES] = head_norm(Z_FK, seg_pair)
    rows_ref[:, R_FOX + LANES:R_FOX + 2 * LANES] = z[:, Z_FV:Z_FV + LANES]
    rows_ref[:, R_NSA:R_NSA + LANES] = z[:, Z_NKV:Z_NKV + LANES]
    for i in (1, 2):
        lo = Z_NKV + i * LANES
        rows_ref[:, R_NSA + i * LANES:R_NSA + (i + 1) * LANES] = jnp.where(
            lane < HEAD_DIM, head_norm(lo, seg_first), z[:, lo:lo + LANES])
    ckv = z[:, Z_CKV:Z_CKV + LANES]
    rows_ref[:, R_LAT:R_LAT + LANES] = (ckv * lax.rsqrt(jnp.mean(ckv * ckv, axis=-1, keepdims=True) + EPS)
                                        * gain[:, Z_CKV:Z_CKV + LANES])
    zg = z[:, Z_G:Z_G + LANES]
    gg = gain[:, Z_G:Z_G + LANES]
    cg, sg = rope_ref[:, 2 * LANES:3 * LANES], rope_ref[:, 3 * LANES:4 * LANES]
    zk = jnp.where(lane < MLA_ROPE_DIM, zg, 0.0)
    kr = zk * lax.rsqrt(_seg_mean(zk * zk, seg_kr) + EPS) * gg
    half = MLA_ROPE_DIM // 2
    kr = kr * cg + jnp.where(lane < half, pltpu.roll(kr, LANES - half, 1), pltpu.roll(kr, half, 1)) * sg
    a = zg + gg
    logf = jnp.minimum(a, 0.0) - jnp.log(1.0 + jnp.exp(-jnp.abs(a)))
    gate = 1.0 / (1.0 + jnp.exp(-zg))
    rows_ref[:, R_G:R_G + LANES] = jnp.where(
        lane < G_LOGF, kr, jnp.where(lane < G_GATE, logf, jnp.where(lane < G_GATE + 3 * N_HEADS, gate, 0.0)))
    cq = z[:, Z_CQ:Z_CQ + MLA_Q_RANK]
    cqn = (cq * lax.rsqrt(jnp.mean(cq * cq, axis=-1, keepdims=True) + EPS)
           * gain[:, Z_CQ:Z_CQ + MLA_Q_RANK]).astype(BF16)
    qh = _dot(cqn, wuq_ref[...])
    cq_t, sq_t = rope_ref[:, 0:LANES], rope_ref[:, LANES:2 * LANES]
    gmla = gmla_ref[...]
    x1_hi = MLA_NOPE_DIM + half
    for h in range(N_HEADS):
        qs = qh[:, h * LANES:(h + 1) * LANES]
        qs = qs * lax.rsqrt(_seg_mean(qs * qs, seg_mla) + EPS) * gmla
        qs = qs * cq_t + jnp.where(lane < x1_hi, pltpu.roll(qs, LANES - half, 1), pltpu.roll(qs, half, 1)) * sq_t
        qa = _dot(qs.astype(BF16), wuk_ref[h])
        qr = jnp.where(lane < MLA_ROPE_DIM, pltpu.roll(qs, LANES - MLA_NOPE_DIM, 1), 0.0)
        qm_ref[:, 2 * h * LANES:(2 * h + 1) * LANES] = qa.astype(BF16)
        qm_ref[:, (2 * h + 1) * LANES:(2 * h + 2) * LANES] = qr.astype(BF16)


def _seg_matrix(blocks):
    m = np.zeros((LANES, LANES), np.float32)
    for lo, n in blocks:
        m[lo:lo + n, lo:lo + n] = 1.0 / n
    return m


_SEG = np.stack([
    _seg_matrix([(0, 64), (64, 64)]),
    _seg_matrix([(0, 64)]),
    _seg_matrix([(0, MLA_ROPE_DIM)]),
    _seg_matrix([(0, MLA_NOPE_DIM), (MLA_NOPE_DIM, MLA_ROPE_DIM)]),
])


def _layer_proj_params(lw):
    w = lw['w_in']
    o = np.cumsum([0, 512, 128, 128, 8, 512, 384, 24, 384, 128, 32])
    fq, fk, fv, ff, nq, nkv, ng, mcq, mckv, mkr = [w[:, o[i]:o[i + 1]] for i in range(10)]
    zeros = lambda n: jnp.zeros((D_MODEL, n), F32)
    fq_cols = []
    for h in range(N_HEADS):
        part = fq[:, h * HEAD_DIM:(h + 1) * HEAD_DIM]
        fq_cols += [part, zeros(HEAD_DIM)] if h < FOX_GROUP else [zeros(HEAD_DIM), part]
    wp = jnp.concatenate(fq_cols + [nq, fk, fv, nkv, mckv, mcq, mkr, ff, ng, zeros(LANES - 64)], axis=1)
    one = lambda n: jnp.ones((n,), F32)
    zero = lambda n: jnp.zeros((n,), F32)
    scale = HEAD_DIM ** -0.5
    fq_gain = []
    for h in range(N_HEADS):
        fq_gain += [lw['fox_qn'] * scale, zero(HEAD_DIM)] if h < FOX_GROUP else [zero(HEAD_DIM), lw['fox_qn'] * scale]
    gain = jnp.concatenate(fq_gain + [jnp.tile(lw['nsa_qn'] * scale, N_HEADS), jnp.tile(lw['fox_kn'], 2), one(LANES),
                                      one(LANES), lw['nsa_kn'][1], one(64), lw['nsa_kn'][2], one(64),
                                      lw['mla_ckvn'], lw['mla_cqn'], lw['mla_kn'][MLA_NOPE_DIM:], lw['fox_fb'],
                                      zero(LANES - 40)])[None, :]
    wuq = lw['mla_wuq'].reshape(MLA_Q_RANK, N_HEADS, MLA_QK_DIM)
    wuq = jnp.pad(wuq, ((0, 0), (0, 0), (0, LANES - MLA_QK_DIM))).reshape(MLA_Q_RANK, N_HEADS * LANES)
    gmla = jnp.concatenate([lw['mla_qn'], zero(LANES - MLA_QK_DIM)])[None, :]
    wukv = lw['mla_wukv'].reshape(MLA_KV_RANK, N_HEADS, MLA_NOPE_DIM + MLA_V_DIM)
    wuk = wukv[:, :, :MLA_NOPE_DIM]
    wuv = wukv[:, :, MLA_NOPE_DIM:]
    kgain = lw['mla_kn'][:MLA_NOPE_DIM]
    wuk_abs = jnp.transpose(wuk, (1, 2, 0)) * kgain[None, :, None]
    wuk_abs = jnp.pad(wuk_abs, ((0, 0), (0, LANES - MLA_NOPE_DIM), (0, 0)))
    return dict(w=wp.astype(BF16), gain=gain, gmix=lw['norm_mix'][None, :], wuq=wuq.astype(BF16), gmla=gmla,
                wuk_abs=wuk_abs.astype(BF16),
                wuk_t=jnp.transpose(wuk, (1, 2, 0)).reshape(N_HEADS * MLA_NOPE_DIM, MLA_KV_RANK).astype(BF16),
                wuv=jnp.transpose(wuv, (1, 0, 2)).astype(BF16))


def _rope_tables(pos):
    inv = ROPE_BASE ** (-jnp.arange(0, MLA_ROPE_DIM, 2, dtype=F32) / MLA_ROPE_DIM)
    ang = pos.astype(F32)[:, None] * inv
    cos, sin = jnp.cos(ang), jnp.sin(ang)
    n = pos.shape[0]
    one = lambda k: jnp.ones((n, k), F32)
    zero = lambda k: jnp.zeros((n, k), F32)
    half = MLA_ROPE_DIM // 2
    rest = LANES - MLA_NOPE_DIM - MLA_ROPE_DIM
    cq = jnp.concatenate([one(MLA_NOPE_DIM), cos, cos, one(rest)], axis=1)
    sq = jnp.concatenate([zero(MLA_NOPE_DIM), -sin, sin, zero(rest)], axis=1)
    cg = jnp.concatenate([cos, cos, one(LANES - 2 * half)], axis=1)
    sg = jnp.concatenate([-sin, sin, zero(LANES - 2 * half)], axis=1)
    return jnp.concatenate([cq, sq, cg, sg], axis=1)


def _project(x, rope, pp, tm=256):
    t = x.shape[0]
    tm = min(tm, t)
    assert t % tm == 0
    const = lambda shape: pl.BlockSpec(shape, lambda i: (0,) * len(shape))
    tok = lambda w: pl.BlockSpec((tm, w), lambda i: (i, 0))
    return pl.pallas_call(
        _proj_kernel,
        grid=(t // tm,),
        in_specs=[tok(D_MODEL), const((1, D_MODEL)), const((D_MODEL, Z_END)), const((1, Z_END)),
                  const((4, LANES, LANES)), const((MLA_Q_RANK, N_HEADS * LANES)), const((1, LANES)),
                  const((N_HEADS, LANES, LANES)), tok(4 * LANES)],
        out_specs=[tok(N_HEADS * LANES), tok(N_HEADS * HEAD_DIM), tok(2 * N_HEADS * LANES), tok(ROWS_W)],
        out_shape=[jax.ShapeDtypeStruct((t, N_HEADS * LANES), BF16),
                   jax.ShapeDtypeStruct((t, N_HEADS * HEAD_DIM), BF16),
                   jax.ShapeDtypeStruct((t, 2 * N_HEADS * LANES), BF16),
                   jax.ShapeDtypeStruct((t, ROWS_W), F32)],
        compiler_params=_cparams(("parallel",)),
        name="proj",
    )(x, pp['gmix'], pp['w'], pp['gain'], jnp.asarray(_SEG, BF16), pp['wuq'], pp['gmla'], pp['wuk_abs'], rope)


POS_PER_ROW = LANES // N_HEADS


def _scan_body(x, buf_ref, seg_len, reverse):
    r = x.shape[0]
    lane = _lane(x.shape)
    pos = lax.broadcasted_iota(I32, x.shape, 0) * POS_PER_ROW + lane // N_HEADS
    segpos = pos % seg_len
    buf_ref[...] = jnp.zeros(buf_ref.shape, F32)

    def row_shift(a, k):
        if k == 0:
            return a
        buf_ref[pl.ds(r, r), :] = a
        return buf_ref[pl.ds(r - k, r), :]

    def shift(y, sh):
        lane_sh, row_sh = (sh % POS_PER_ROW) * N_HEADS, sh // POS_PER_ROW
        sign = -1 if reverse else 1
        if lane_sh == 0:
            return row_shift(y, sign * row_sh)
        a = pltpu.roll(y, (LANES - lane_sh) if reverse else lane_sh, 1)
        same = (lane < LANES - lane_sh) if reverse else (lane >= lane_sh)
        return jnp.where(same, row_shift(a, sign * row_sh), row_shift(a, sign * (row_sh + 1)))

    y = x
    sh = 1
    while sh < seg_len and sh // POS_PER_ROW < r:
        valid = (segpos + sh < seg_len) if reverse else (segpos >= sh)
        y = y + jnp.where(valid, shift(y, sh), 0.0)
        sh *= 2
    return (x - y) if reverse else y


def _scan_kernel(x_ref, o_ref, buf_ref, *, seg_len, reverse):
    o_ref[0] = _scan_body(x_ref[0], buf_ref, seg_len, reverse)


def _scan(x, seg_len, reverse=False):
    n, p_in, h = x.shape
    assert h == N_HEADS
    p = -(-p_in // (POS_PER_ROW * SUBLANES)) * (POS_PER_ROW * SUBLANES)
    x = jnp.pad(x, ((0, 0), (0, p - p_in), (0, 0)))
    r = p // POS_PER_ROW
    out = pl.pallas_call(
        functools.partial(_scan_kernel, seg_len=seg_len, reverse=reverse),
        grid=(n,),
        in_specs=[pl.BlockSpec((1, r, LANES), lambda i: (i, 0, 0))],
        out_specs=pl.BlockSpec((1, r, LANES), lambda i: (i, 0, 0)),
        out_shape=jax.ShapeDtypeStruct((n, r, LANES), F32),
        scratch_shapes=[pltpu.VMEM((3 * r, LANES), F32)],
        compiler_params=_cparams(("parallel",)),
        name="logf_scan",
    )(x.reshape(n, r, LANES))
    return out.reshape(n, p, h)[:, :p_in]


def _online(s, v, m, l, acc):
    m_new = jnp.maximum(m, jnp.max(s, axis=-1, keepdims=True))
    alpha = jnp.exp(m - m_new)
    p = jnp.exp(s - m_new)
    l = alpha * l + jnp.sum(p, axis=-1, keepdims=True)
    acc = alpha * acc + _dot(p.astype(BF16), v)
    return m_new, l, acc


def _rows3(x):
    return x.reshape(x.shape[0] // N_HEADS, N_HEADS, x.shape[1])


def _fox_prompt_kernel(q_ref, c_ref, ct_ref, kv_ref, o_ref):
    i = pl.program_id(1)
    q = q_ref[0]
    rows = q.shape[0]
    cq = c_ref[0]

    def tile(j, carry, diag):
        m, l, acc = carry
        kv = kv_ref[0, pl.ds(pl.multiple_of(j * Q_TILE, Q_TILE), Q_TILE), :]
        k, v = kv[:, :LANES].astype(BF16), kv[:, LANES:].astype(BF16)
        ck = ct_ref[0, :, pl.ds(pl.multiple_of(j * Q_TILE, Q_TILE), Q_TILE)]
        s = _rows3(_dot_nt(q, k)) + (_rows3(jnp.broadcast_to(cq, (rows, Q_TILE))) - ck[None])
        if diag:
            tq = lax.broadcasted_iota(I32, s.shape, 0)
            tk = lax.broadcasted_iota(I32, s.shape, 2)
            s = jnp.where(tk <= tq, s, NEG)
        return _online(s.reshape(rows, Q_TILE), v, m, l, acc)

    init = (jnp.full((rows, 1), NEG, F32), jnp.zeros((rows, 1), F32), jnp.zeros((rows, LANES), F32))
    carry = lax.fori_loop(0, i, lambda j, c: tile(j, c, False), init)
    m, l, acc = tile(i, carry, True)
    o = acc / jnp.maximum(l, 1e-30)
    head = lax.broadcasted_iota(I32, (rows, HEAD_DIM), 0) % N_HEADS
    o_ref[0] = jnp.where(head < FOX_GROUP, o[:, :HEAD_DIM], o[:, HEAD_DIM:]).astype(BF16)


def _fox_prompt(qf, rows, c):
    b, s, _ = qf.shape
    nq = s // Q_TILE
    out = pl.pallas_call(
        _fox_prompt_kernel,
        grid=(b, nq),
        in_specs=[pl.BlockSpec((1, Q_TILE * N_HEADS, LANES), lambda bi, i: (bi, i, 0)),
                  pl.BlockSpec((1, Q_TILE * N_HEADS, 1), lambda bi, i: (bi, i, 0)),
                  pl.BlockSpec((1, N_HEADS, s), lambda bi, i: (bi, 0, 0)),
                  pl.BlockSpec((1, s, 2 * LANES), lambda bi, i: (bi, 0, R_FOX // (2 * LANES)))],
        out_specs=pl.BlockSpec((1, Q_TILE * N_HEADS, HEAD_DIM), lambda bi, i: (bi, i, 0)),
        out_shape=jax.ShapeDtypeStruct((b, s * N_HEADS, HEAD_DIM), BF16),
        compiler_params=_cparams(("parallel", "parallel")),
        name="fox_prompt",
    )(qf.reshape(b, s * N_HEADS, LANES), c.reshape(b, s * N_HEADS, 1), jnp.transpose(c, (0, 2, 1)), rows)
    return out.reshape(b, s, N_HEADS * HEAD_DIM)


def _mla_rs_kernel(lat_ref, wuk_ref, red_ref, o_ref):
    k = _dot(lat_ref[...].astype(BF16), wuk_ref[...])
    o_ref[...] = lax.rsqrt(_seg_mean(k * k, red_ref[...]) + EPS)


_HEAD_REDUCE = np.zeros((N_HEADS * MLA_NOPE_DIM, LANES), np.float32)
for _h in range(N_HEADS):
    _HEAD_REDUCE[_h * MLA_NOPE_DIM:(_h + 1) * MLA_NOPE_DIM, _h] = 1.0 / MLA_NOPE_DIM


def _mla_rs(rows2d, wuk_t, tm=512):
    t = rows2d.shape[0]
    tm = min(tm, t)
    out = pl.pallas_call(
        _mla_rs_kernel,
        grid=(t // tm,),
        in_specs=[pl.BlockSpec((tm, LANES), lambda i: (i, R_LAT // LANES)),
                  pl.BlockSpec((MLA_KV_RANK, N_HEADS * MLA_NOPE_DIM), lambda i: (0, 0)),
                  pl.BlockSpec((N_HEADS * MLA_NOPE_DIM, LANES), lambda i: (0, 0))],
        out_specs=pl.BlockSpec((tm, LANES), lambda i: (i, 0)),
        out_shape=jax.ShapeDtypeStruct((t, LANES), F32),
        compiler_params=_cparams(("parallel",)),
        name="mla_rs",
    )(rows2d, wuk_t.T, jnp.asarray(_HEAD_REDUCE, BF16))
    return out[:, :N_HEADS]


def _mla_finish(acc, l, wuv_ref, o_ref, stage_ref):
    stage_ref[...] = acc / jnp.maximum(l, 1e-30)
    n_tok = acc.shape[0] // N_HEADS
    for h in range(N_HEADS):
        oh = stage_ref[pl.ds(h, n_tok, stride=N_HEADS), :]
        o_ref[0, :, h * MLA_V_DIM:(h + 1) * MLA_V_DIM] = _dot(oh.astype(BF16), wuv_ref[h]).astype(o_ref.dtype)


def _mla_prompt_kernel(q_ref, rst_ref, lat_ref, g_ref, wuv_ref, o_ref, stage_ref):
    i = pl.program_id(1)
    q = q_ref[0]
    rows = q.shape[0]
    qa, qr = q[:, :LANES], q[:, LANES:]
    scale = MLA_QK_DIM ** -0.5

    def tile(j, carry, diag):
        m, l, acc = carry
        sl = pl.ds(pl.multiple_of(j * Q_TILE, Q_TILE), Q_TILE)
        lat = lat_ref[0, sl, :].astype(BF16)
        kr = g_ref[0, sl, :].astype(BF16)
        s = (_rows3(_dot_nt(qa, lat)) * rst_ref[0, :, sl][None] + _rows3(_dot_nt(qr, kr))) * scale
        if diag:
            tq = lax.broadcasted_iota(I32, s.shape, 0)
            tk = lax.broadcasted_iota(I32, s.shape, 2)
            s = jnp.where(tk <= tq, s, NEG)
        return _online(s.reshape(rows, Q_TILE), lat, m, l, acc)

    init = (jnp.full((rows, 1), NEG, F32), jnp.zeros((rows, 1), F32), jnp.zeros((rows, LANES), F32))
    carry = lax.fori_loop(0, i, lambda j, c: tile(j, c, False), init)
    m, l, acc = tile(i, carry, True)
    _mla_finish(acc, l, wuv_ref, o_ref, stage_ref)


def _mla_prompt(qm, rows, rs, wuv):
    b, s, _ = qm.shape
    nq = s // Q_TILE
    return pl.pallas_call(
        _mla_prompt_kernel,
        grid=(b, nq),
        in_specs=[pl.BlockSpec((1, Q_TILE * N_HEADS, 2 * LANES), lambda bi, i: (bi, i, 0)),
                  pl.BlockSpec((1, N_HEADS, s), lambda bi, i: (bi, 0, 0)),
                  pl.BlockSpec((1, s, LANES), lambda bi, i: (bi, 0, R_LAT // LANES)),
                  pl.BlockSpec((1, s, LANES), lambda bi, i: (bi, 0, R_G // LANES)),
                  pl.BlockSpec((N_HEADS, MLA_KV_RANK, MLA_V_DIM), lambda bi, i: (0, 0, 0))],
        out_specs=pl.BlockSpec((1, Q_TILE, N_HEADS * MLA_V_DIM), lambda bi, i: (bi, i, 0)),
        out_shape=jax.ShapeDtypeStruct((b, s, N_HEADS * MLA_V_DIM), BF16),
        scratch_shapes=[pltpu.VMEM((Q_TILE * N_HEADS, LANES), F32)],
        compiler_params=_cparams(("parallel", "parallel")),
        name="mla_prompt",
    )(qm.reshape(b, s * N_HEADS, 2 * LANES), jnp.transpose(rs, (0, 2, 1)), rows, rows, wuv)


def _compress_body(x_ref, phi_ref, cpos_ref, kn_ref, seg_ref, n_blocks):
    def step(j, acc):
        xj = x_ref[pl.ds(j, n_blocks, stride=NSA_BLOCK), :] + cpos_ref[pl.ds(j, 1), :]
        return acc + _dot(xj.astype(BF16), phi_ref[j])
    kv = lax.fori_loop(0, NSA_BLOCK, step, jnp.zeros((n_blocks, LANES), F32))
    lane = _lane(kv.shape)
    kc = jnp.where(lane < HEAD_DIM, kv, 0.0)
    kc = kc * lax.rsqrt(_seg_mean(kc * kc, seg_ref[...]) + EPS) * kn_ref[...]
    return jnp.where(lane < HEAD_DIM, kc, kv)


def _compress_params(lw):
    phi = lw['nsa_phi']
    z = jnp.zeros_like(phi[0])
    phi_j = jnp.concatenate([jnp.concatenate([phi[0], z], axis=2), jnp.concatenate([z, phi[1]], axis=2)], axis=1)
    cpos = jnp.concatenate([lw['nsa_cpos'][0], lw['nsa_cpos'][1]], axis=1)
    kn = jnp.concatenate([lw['nsa_kn'][0], jnp.ones((HEAD_DIM,), F32)])[None, :]
    return dict(phi=phi_j.astype(BF16), cpos=cpos, kn=kn, seg=jnp.asarray(_SEG[1], BF16))


def _compress_prompt_kernel(x_ref, phi_ref, cpos_ref, kn_ref, seg_ref, o_ref, *, n_blocks):
    o_ref[0] = jnp.zeros(o_ref.shape[1:], F32)
    o_ref[0, 0:n_blocks, :] = _compress_body(x_ref.at[0], phi_ref, cpos_ref, kn_ref, seg_ref, n_blocks)


def _compress_prompt(rows, cp, nbp):
    b, s, _ = rows.shape
    n_blocks = s // NSA_BLOCK
    const = lambda shape: pl.BlockSpec(shape, lambda i: (0,) * len(shape))
    return pl.pallas_call(
        functools.partial(_compress_prompt_kernel, n_blocks=n_blocks),
        grid=(b,),
        in_specs=[pl.BlockSpec((1, s, LANES), lambda i: (i, 0, R_NSA // LANES)),
                  const((NSA_BLOCK, LANES, LANES)), const((NSA_BLOCK, LANES)), const((1, LANES)),
                  const((LANES, LANES))],
        out_specs=pl.BlockSpec((1, nbp, LANES), lambda i: (i, 0, 0)),
        out_shape=jax.ShapeDtypeStruct((b, nbp, LANES), F32),
        compiler_params=_cparams(("parallel",)),
        name="nsa_compress_prompt",
    )(rows, cp['phi'], cp['cpos'], cp['kn'], cp['seg'])


def _cmp_attn_kernel(q_ref, kv_ref, bias_ref, forced_ref, oc_ref, imp_ref):
    q = q_ref[0]
    kv = kv_ref[0].astype(BF16)
    bias = bias_ref[...]
    vis = bias > 0.5 * NEG
    s = jnp.where(vis, _dot_nt(q, kv[:, :HEAD_DIM]) + bias, NEG)
    p = jnp.where(vis, jnp.exp(s - jnp.max(s, axis=-1, keepdims=True)), 0.0)
    p = p / jnp.maximum(jnp.sum(p, axis=-1, keepdims=True), 1e-30)
    oc_ref[0] = _dot(p.astype(BF16), kv)[:, HEAD_DIM:]
    imp = jnp.sum(_rows3(p), axis=1)
    forced = forced_ref[...]
    imp_ref[0] = jnp.where(forced == 0.0, imp, forced)


def _cmp_attn(q, kvc, bias, forced, n_tok):
    g, r, _ = q.shape
    nbp = kvc.shape[1]
    rows = n_tok * N_HEADS
    steps = r // rows
    per_group_bias = bias.shape[0] == r
    return pl.pallas_call(
        _cmp_attn_kernel,
        grid=(g, steps),
        in_specs=[pl.BlockSpec((1, rows, HEAD_DIM), lambda gi, i: (gi, i, 0)),
                  pl.BlockSpec((1, nbp, LANES), lambda gi, i: (gi, 0, 0)),
                  pl.BlockSpec((rows, nbp), (lambda gi, i: (i, 0)) if per_group_bias else (lambda gi, i: (0, 0))),
                  pl.BlockSpec((n_tok, nbp), (lambda gi, i: (i, 0)) if per_group_bias else (lambda gi, i: (0, 0)))],
        out_specs=[pl.BlockSpec((1, rows, HEAD_DIM), lambda gi, i: (gi, i, 0)),
                   pl.BlockSpec((1, n_tok, nbp), lambda gi, i: (gi, i, 0))],
        out_shape=[jax.ShapeDtypeStruct((g, r, HEAD_DIM), F32),
                   jax.ShapeDtypeStruct((g, r // N_HEADS, nbp), F32)],
        compiler_params=_cparams(("parallel", "parallel")),
        name="nsa_cmp_attn",
    )(q, kvc, bias, forced)


def _topk_kernel(imp_ref, pen_ref, idx_ref):
    x = imp_ref[...]
    lane = _lane(x.shape)
    sel = jnp.zeros(x.shape, jnp.bool_)
    idx = jnp.zeros(x.shape, I32)
    for k in range(NSA_TOPK):
        mx = jnp.max(x, axis=-1, keepdims=True)
        first = jnp.min(jnp.where(x == mx, lane, x.shape[-1]), axis=-1, keepdims=True)
        chosen = lane == first
        sel = jnp.logical_or(sel, chosen)
        idx = jnp.where(lane == k, first, idx)
        x = jnp.where(chosen, -jnp.inf, x)
    pen_ref[...] = jnp.where(sel, 0.0, NEG)
    idx_ref[...] = idx[:, :LANES]


def _topk(imp, tm=256):
    t, nbp = imp.shape
    tm = min(tm, t)
    pen, idx = pl.pallas_call(
        _topk_kernel,
        grid=(t // tm,),
        in_specs=[pl.BlockSpec((tm, nbp), lambda i: (i, 0))],
        out_specs=[pl.BlockSpec((tm, nbp), lambda i: (i, 0)), pl.BlockSpec((tm, LANES), lambda i: (i, 0))],
        out_shape=[jax.ShapeDtypeStruct((t, nbp), F32), jax.ShapeDtypeStruct((t, LANES), I32)],
        compiler_params=_cparams(("parallel",)),
        name="nsa_topk",
    )(imp)
    return pen, idx[:, :NSA_TOPK]


def _rel_bucket(dist):
    exact = REL_BUCKETS // 2
    n = jnp.maximum(dist, 0)
    log_ratio = jnp.log(jnp.maximum(n, 1).astype(F32) / exact) / math.log(REL_MAX_DIST / exact)
    large = jnp.minimum(exact + (log_ratio * (REL_BUCKETS - exact)).astype(I32), REL_BUCKETS - 1)
    return jnp.where(n < exact, n, large)


def _bias_by_dist(rel_bias):
    return rel_bias[_rel_bucket(jnp.arange(REL_MAX_DIST + 1, dtype=I32))]


def _bias_lookup(tbl, dist, visible):
    b = tbl[jnp.clip(dist, 0, REL_MAX_DIST)]
    b = jnp.where(visible[..., None], b, NEG)
    return jnp.transpose(b, (0, 2, 1)).reshape(dist.shape[0] * N_HEADS, dist.shape[1])


def _forced_code(qpos, nbp):
    bq = (qpos // NSA_BLOCK)[:, None]
    bidx = jnp.arange(nbp, dtype=I32)[None, :]
    forced = (bidx == 0) | (bidx == bq) | (bidx == bq - 1)
    return jnp.where(bidx <= bq, jnp.where(forced, NSA_FORCED_SCORE, 0.0), NEG).astype(F32)


def _nsa_prompt_kernel(qa_ref, slc_ref, win_ref, bt_ref, cf_ref, oc_ref, g_ref, o_ref):
    i = pl.program_id(1)
    qa = qa_ref[0]
    rows = qa.shape[0]
    lane_q = _lane(qa.shape)
    q0 = jnp.where(lane_q < HEAD_DIM, qa, jnp.zeros_like(qa))
    cf = cf_ref[...]
    key_row = lax.broadcasted_iota(I32, (Q_TILE, LANES), 0)
    key_lane = lax.broadcasted_iota(I32, (Q_TILE, LANES), 1)

    def sel_tile(j, carry, bias):
        sl = pl.ds(pl.multiple_of(j * Q_TILE, Q_TILE), Q_TILE)
        kv = slc_ref[0, sl, :]
        block = (j * Q_TILE + key_row) // NSA_BLOCK
        k = jnp.where(key_lane < HEAD_DIM, kv, (key_lane - HEAD_DIM == block).astype(F32)).astype(BF16)
        return _online(_dot_nt(qa, k) + bias, kv.astype(BF16), *carry)

    def win_tile(j, carry, bias):
        sl = pl.ds(pl.multiple_of(j * Q_TILE, Q_TILE), Q_TILE)
        kv = win_ref[0, sl, :].astype(BF16)
        return _online(_dot_nt(q0, kv) + bias, kv, *carry)

    def maybe(pred, fn, carry):
        return lax.cond(pred, fn, lambda c: c, carry)

    init = (jnp.full((rows, 1), NEG, F32), jnp.zeros((rows, 1), F32), jnp.zeros((rows, LANES), F32))
    cs = lax.fori_loop(0, jnp.maximum(i - 1, 0), lambda j, c: sel_tile(j, c, cf), init)
    cs = maybe(i >= 1, lambda c: sel_tile(i - 1, c, bt_ref[1]), cs)
    ms, ls, accs = sel_tile(i, cs, bt_ref[0])
    n_far = NSA_WINDOW // Q_TILE
    tq = lax.broadcasted_iota(I32, (rows, Q_TILE), 0) // N_HEADS
    tk = lax.broadcasted_iota(I32, (rows, Q_TILE), 1)
    edge = jnp.where(tk >= tq, cf, NEG)
    cw = maybe(i >= n_far, lambda c: win_tile(i - n_far, c, edge), init)
    for o in range(n_far - 1, 1, -1):
        cw = maybe(i >= o, lambda c, o=o: win_tile(i - o, c, cf), cw)
    cw = maybe(i >= 1, lambda c: win_tile(i - 1, c, bt_ref[1]), cw)
    mw, lw_, accw = win_tile(i, cw, bt_ref[0])
    g = g_ref[0]
    o_s = accs[:, HEAD_DIM:] / jnp.maximum(ls, 1e-30)
    o_w = accw[:, HEAD_DIM:] / jnp.maximum(lw_, 1e-30)
    o_ref[0] = (g[:, 0:1] * oc_ref[0] + g[:, 1:2] * o_s + g[:, 2:3] * o_w).astype(BF16)


def _nsa_prompt(qa, rows, bias_tiles, bias_far, oc, gates):
    b, r, _ = qa.shape
    s = r // N_HEADS
    rt = Q_TILE * N_HEADS
    return pl.pallas_call(
        _nsa_prompt_kernel,
        grid=(b, s // Q_TILE),
        in_specs=[pl.BlockSpec((1, rt, LANES), lambda bi, i: (bi, i, 0)),
                  pl.BlockSpec((1, s, LANES), lambda bi, i: (bi, 0, (R_NSA + LANES) // LANES)),
                  pl.BlockSpec((1, s, LANES), lambda bi, i: (bi, 0, R_WIN // LANES)),
                  pl.BlockSpec((2, rt, LANES), lambda bi, i: (0, 0, 0)),
                  pl.BlockSpec((rt, 1), lambda bi, i: (0, 0)),
                  pl.BlockSpec((1, rt, HEAD_DIM), lambda bi, i: (bi, i, 0)),
                  pl.BlockSpec((1, rt, 3), lambda bi, i: (bi, i, 0))],
        out_specs=pl.BlockSpec((1, rt, HEAD_DIM), lambda bi, i: (bi, i, 0)),
        out_shape=jax.ShapeDtypeStruct((b, r, HEAD_DIM), BF16),
        compiler_params=_cparams(("parallel", "parallel")),
        name="nsa_prompt",
    )(qa, rows, rows, bias_tiles, bias_far, oc, gates)


def _nsa_prompt_branch(qn, rows, cp, tbl):
    b, s, _ = qn.shape
    n_blocks = s // NSA_BLOCK
    assert s % Q_TILE == 0 and n_blocks <= HEAD_DIM and n_blocks >= NSA_TOPK
    nbp = LANES
    pos = jnp.arange(s, dtype=I32)
    kvc = _compress_prompt(rows, cp, nbp)
    cend = jnp.arange(nbp, dtype=I32) * NSA_BLOCK + (NSA_BLOCK - 1)
    dist_c = pos[:, None] - cend[None, :]
    bias_c = _bias_lookup(tbl, dist_c, (dist_c >= 0) & (jnp.arange(nbp) < n_blocks)[None, :])
    q_rows = qn.reshape(b, s * N_HEADS, HEAD_DIM)
    oc, imp = _cmp_attn(q_rows, kvc, bias_c, _forced_code(pos, nbp), Q_TILE)
    pen, _ = _topk(imp.reshape(b * s, nbp))
    pen = jnp.broadcast_to(pen[:, None, :HEAD_DIM], (b * s, N_HEADS, HEAD_DIM)).astype(BF16)
    qa = jnp.concatenate([q_rows, pen.reshape(b, s * N_HEADS, HEAD_DIM)], axis=-1)
    t_in = jnp.arange(Q_TILE, dtype=I32)
    d0 = t_in[:, None] - t_in[None, :]
    bias_tiles = jnp.stack([_bias_lookup(tbl, d0, d0 >= 0), _bias_lookup(tbl, d0 + Q_TILE, d0 > -Q_TILE)])
    bias_far = jnp.tile(tbl[REL_MAX_DIST], Q_TILE)[:, None]
    g = rows[:, :, R_G + G_GATE:R_G + G_GATE + 3 * N_HEADS].reshape(b, s, 3, N_HEADS)
    g = jnp.transpose(g, (0, 1, 3, 2)).reshape(b, s * N_HEADS, 3)
    out = _nsa_prompt(qa, rows, bias_tiles, bias_far, oc, g)
    return out.reshape(b, s, N_HEADS * HEAD_DIM)


def _attn_out_kernel(x_ref, fox_ref, nsa_ref, mla_ref, w_ref, o_ref):
    w = N_HEADS * HEAD_DIM
    o_ref[...] = (x_ref[...] + _dot(fox_ref[...], w_ref[0:w, :]) + _dot(nsa_ref[...], w_ref[w:2 * w, :])
                  + _dot(mla_ref[...], w_ref[2 * w:, :]))


def _attn_out(x, fox, nsa, mla, w_out, tm=512):
    t = x.shape[0]
    tm = min(tm, t)
    tok = lambda w: pl.BlockSpec((tm, w), lambda i: (i, 0))
    return pl.pallas_call(
        _attn_out_kernel,
        grid=(t // tm,),
        in_specs=[tok(D_MODEL), tok(fox.shape[1]), tok(nsa.shape[1]), tok(mla.shape[1]),
                  pl.BlockSpec((D_MODEL, D_MODEL), lambda i: (0, 0))],
        out_specs=tok(D_MODEL),
        out_shape=jax.ShapeDtypeStruct((t, D_MODEL), F32),
        compiler_params=_cparams(("parallel",)),
        name="attn_out",
    )(x, fox, nsa, mla, w_out)


FF_TILE = 512


def _ffn_kernel(*refs, period, tiles_per_seq):
    if period is None:
        h_ref, g_ref, wa_ref, wb_ref, cw_ref, wd_ref, o_ref, tail_ref, hn_ref, acc_ref, ext_ref, halo_ref = refs
    else:
        h_ref, g_ref, wa_ref, wb_ref, cw_ref, wd_ref, p1_ref, p2_ref, o_ref, a_ref, hn_ref, acc_ref, ext_ref = refs
    i, j = pl.program_id(0), pl.program_id(1)
    tm = h_ref.shape[0]

    @pl.when(j == 0)
    def _():
        h = h_ref[...]
        hn_ref[...] = (h * lax.rsqrt(jnp.mean(h * h, axis=-1, keepdims=True) + EPS) * g_ref[...]).astype(BF16)
        acc_ref[...] = h

    hn = hn_ref[...]
    a = _dot(hn, wa_ref[...])
    b = _dot(hn, wb_ref[...])
    ext_ref[pl.ds(SUBLANES, tm), :] = a
    if period is None:
        @pl.when(i % tiles_per_seq == 0)
        def _():
            halo_ref[j] = jnp.zeros(halo_ref.shape[1:], F32)
        ext_ref[pl.ds(0, SUBLANES), :] = halo_ref[j]
        halo_ref[j] = a[tm - SUBLANES:, :]
        tail_ref[0] = a[tm - SUBLANES:, :]
        a1 = ext_ref[pl.ds(SUBLANES - 1, tm), :]
        a2 = ext_ref[pl.ds(SUBLANES - 2, tm), :]
    else:
        ext_ref[pl.ds(0, SUBLANES), :] = jnp.zeros((SUBLANES, a.shape[1]), F32)
        a_ref[...] = a
        t_in = lax.broadcasted_iota(I32, a.shape, 0) % period
        a1 = jnp.where(t_in >= 1, ext_ref[pl.ds(SUBLANES - 1, tm), :], p1_ref[...])
        a2 = jnp.where(t_in >= 2, ext_ref[pl.ds(SUBLANES - 2, tm), :], p2_ref[...])
    cw = cw_ref[...]
    c = cw[0:1, :] * a2 + cw[1:2, :] * a1 + cw[2:3, :] * a + cw[3:4, :]
    gated = (c / (1.0 + jnp.exp(-c))) * b
    acc_ref[...] += _dot(gated.astype(BF16), wd_ref[...])

    @pl.when(j == pl.num_programs(1) - 1)
    def _():
        o_ref[...] = acc_ref[...]


def _ffn(h, fw, seq_len=None, prev=None, tm=512):
    t = h.shape[0]
    tm = min(tm, t)
    nj = D_FF // FF_TILE
    tokb = lambda w: pl.BlockSpec((tm, w), lambda i, j: (i, 0))
    ffb = pl.BlockSpec((tm, FF_TILE), lambda i, j: (i, j))
    in_specs = [tokb(D_MODEL), pl.BlockSpec((1, D_MODEL), lambda i, j: (0, 0)),
                pl.BlockSpec((D_MODEL, FF_TILE), lambda i, j: (0, j)),
                pl.BlockSpec((D_MODEL, FF_TILE), lambda i, j: (0, j + nj)),
                pl.BlockSpec((SUBLANES, FF_TILE), lambda i, j: (0, j)),
                pl.BlockSpec((FF_TILE, D_MODEL), lambda i, j: (j, 0))]
    args = [h, fw['norm'], fw['up'], fw['up'], fw['conv'], fw['down']]
    scratch = [pltpu.VMEM((tm, D_MODEL), BF16), pltpu.VMEM((tm, D_MODEL), F32),
               pltpu.VMEM((tm + SUBLANES, FF_TILE), F32)]
    if prev is None:
        tiles_per_seq = seq_len // tm
        period = None
        extra_spec = pl.BlockSpec((1, SUBLANES, FF_TILE), lambda i, j: (i, 0, j))
        extra_shape = jax.ShapeDtypeStruct((t // tm, SUBLANES, D_FF), F32)
        scratch.append(pltpu.VMEM((nj, SUBLANES, FF_TILE), F32))
    else:
        tiles_per_seq = None
        period = seq_len
        in_specs += [ffb, ffb]
        args += list(prev)
        extra_spec = ffb
        extra_shape = jax.ShapeDtypeStruct((t, D_FF), F32)
    return pl.pallas_call(
        functools.partial(_ffn_kernel, period=period, tiles_per_seq=tiles_per_seq),
        grid=(t // tm, nj),
        in_specs=in_specs,
        out_specs=[tokb(D_MODEL), extra_spec],
        out_shape=[jax.ShapeDtypeStruct((t, D_MODEL), F32), extra_shape],
        scratch_shapes=scratch,
        compiler_params=_cparams(("arbitrary", "arbitrary")),
        name="ffn",
    )(*args)


def _ple_kernel(h_ref, g_ref, wg_ref, p_ref, wp_ref, o_ref):
    h = h_ref[...]
    hn = (h * lax.rsqrt(jnp.mean(h * h, axis=-1, keepdims=True) + EPS) * g_ref[...]).astype(BF16)
    gate = 1.0 / (1.0 + jnp.exp(-_dot(hn, wg_ref[...])))
    o_ref[...] = h + gate * _dot(p_ref[...].astype(BF16), wp_ref[...])


def _ple(h, p, fw, tm=512):
    t = h.shape[0]
    tm = min(tm, t)
    tok = lambda w: pl.BlockSpec((tm, w), lambda i: (i, 0))
    const = lambda shape: pl.BlockSpec(shape, lambda i: (0,) * len(shape))
    return pl.pallas_call(
        _ple_kernel,
        grid=(t // tm,),
        in_specs=[tok(D_MODEL), const((1, D_MODEL)), const((D_MODEL, D_MODEL)), tok(p.shape[1]),
                  const((p.shape[1], D_MODEL))],
        out_specs=tok(D_MODEL),
        out_shape=jax.ShapeDtypeStruct((t, D_MODEL), F32),
        compiler_params=_cparams(("parallel",)),
        name="ple",
    )(h, fw['ple_norm'], fw['ple_gate'], p, fw['ple_proj'])


def _channel_params(lw):
    conv = jnp.concatenate([lw['ffn_conv'], lw['ffn_conv_b'][None, :],
                            jnp.zeros((SUBLANES - CONV_W - 1, D_FF), F32)], axis=0)
    return dict(w_out=lw['w_out'].astype(BF16), norm=lw['norm_ffn'][None, :], up=lw['ffn_up'].astype(BF16),
                conv=conv, down=lw['ffn_down'].astype(BF16), ple_norm=lw['ple_norm'][None, :],
                ple_gate=lw['ple_gate'].astype(BF16), ple_proj=lw['ple_proj'].astype(BF16))


PAGE = 128
KEY_CHUNK = 512


def _page_copies(pt_ref, seq, n_pages, src_of_page, buf_ref, slot, sem_ref, rows, wait):
    def body(p, carry):
        cp = pltpu.make_async_copy(src_of_page(pt_ref[seq, p]),
                                   buf_ref.at[slot, pl.ds(pl.multiple_of(p * rows, rows), rows)],
                                   sem_ref.at[slot])
        if wait:
            cp.wait()
        else:
            cp.start()
        return carry
    lax.fori_loop(0, n_pages, body, 0)


def _paged_pipeline(pt_ref, n_pages, src_of_page, buf_ref, sem_ref, rows):
    b = pl.program_id(0)
    slot = b % 2

    @pl.when(b == 0)
    def _():
        _page_copies(pt_ref, 0, n_pages, src_of_page, buf_ref, 0, sem_ref, rows, wait=False)

    @pl.when(b + 1 < pl.num_programs(0))
    def _():
        _page_copies(pt_ref, b + 1, n_pages, src_of_page, buf_ref, 1 - slot, sem_ref, rows, wait=False)

    _page_copies(pt_ref, b, n_pages, src_of_page, buf_ref, slot, sem_ref, rows, wait=True)
    return slot


def _paged_call(kernel_fn, n_seq, n_prefetch, in_specs, out_specs, out_shape, scratch, name, args):
    return pl.pallas_call(
        kernel_fn,
        grid_spec=pltpu.PrefetchScalarGridSpec(num_scalar_prefetch=n_prefetch, grid=(n_seq,), in_specs=in_specs,
                                               out_specs=out_specs, scratch_shapes=scratch),
        out_shape=out_shape,
        compiler_params=_cparams(("arbitrary",)),
        name=name,
    )(*args)


def _seq_block(shape):
    return pl.BlockSpec((1,) + shape, lambda b, *_: (b,) + (0,) * len(shape))


def _const_block(shape):
    return pl.BlockSpec(shape, lambda b, *_: (0,) * len(shape))


_ANY = pl.BlockSpec(memory_space=pl.ANY)


def _scan_paged_kernel(pt_ref, cache_ref, o_ref, buf_ref, sem_ref, tmp_ref, *, layer, n_pages):
    rows = PAGE // POS_PER_ROW
    slot = _paged_pipeline(pt_ref, n_pages, lambda page: cache_ref.at[layer, page], buf_ref, sem_ref, rows)
    o_ref[0] = _scan_body(buf_ref[slot], tmp_ref, n_pages * PAGE, True)


def _scan_paged(page_table, cache_logf, layer):
    n_seq, n_pages = page_table.shape
    depth, n_pool = cache_logf.shape[:2]
    rows = PAGE // POS_PER_ROW
    r = n_pages * rows
    out = _paged_call(
        functools.partial(_scan_paged_kernel, layer=layer, n_pages=n_pages), n_seq, 1,
        [_ANY], _seq_block((r, LANES)), jax.ShapeDtypeStruct((n_seq, r, LANES), F32),
        [pltpu.VMEM((2, r, LANES), F32), pltpu.SemaphoreType.DMA((2,)), pltpu.VMEM((3 * r, LANES), F32)],
        "logf_scan_paged", (page_table, cache_logf.reshape(depth, n_pool, rows, LANES)))
    return out.reshape(n_seq, n_pages * PAGE, N_HEADS)


def _new_key_mask(s3, n_new):
    tq = lax.broadcasted_iota(I32, s3.shape, 0)
    tk = lax.broadcasted_iota(I32, s3.shape, 2)
    return (tk <= tq) & (tk < n_new)


def _fox_sample_kernel(pt_ref, q_ref, cq_ref, ctp_ref, ctn_ref, new_ref, cache_ref, o_ref, buf_ref, sem_ref,
                       *, layer, n_pages, n_new):
    slot = _paged_pipeline(pt_ref, n_pages, lambda page: cache_ref.at[layer, page], buf_ref, sem_ref, PAGE)
    q = q_ref[0]
    rows = q.shape[0]
    cq = cq_ref[0]

    def chunk(c, carry):
        sl = pl.ds(pl.multiple_of(c * KEY_CHUNK, KEY_CHUNK), KEY_CHUNK)
        kv = buf_ref[slot, sl, :]
        ck = ctp_ref[0, :, sl]
        s = _rows3(_dot_nt(q, kv[:, :LANES].astype(BF16))) + (_rows3(jnp.broadcast_to(cq, (rows, KEY_CHUNK))) - ck[None])
        return _online(s.reshape(rows, KEY_CHUNK), kv[:, LANES:].astype(BF16), *carry)

    init = (jnp.full((rows, 1), NEG, F32), jnp.zeros((rows, 1), F32), jnp.zeros((rows, LANES), F32))
    carry = lax.fori_loop(0, n_pages * PAGE // KEY_CHUNK, chunk, init)
    kv = new_ref[0]
    s = _rows3(_dot_nt(q, kv[:, :LANES].astype(BF16))) + (_rows3(jnp.broadcast_to(cq, (rows, LANES))) - ctn_ref[0][None])
    s = jnp.where(_new_key_mask(s, n_new), s, NEG)
    m, l, acc = _online(s.reshape(rows, LANES), kv[:, LANES:].astype(BF16), *carry)
    o = acc / jnp.maximum(l, 1e-30)
    head = lax.broadcasted_iota(I32, (rows, HEAD_DIM), 0) % N_HEADS
    o_ref[0] = jnp.where(head < FOX_GROUP, o[:, :HEAD_DIM], o[:, HEAD_DIM:]).astype(BF16)


def _pad_keys(x):
    return jnp.pad(x, ((0, 0), (0, LANES - x.shape[1]), (0, 0)))


def _fox_sample(page_table, qf, rows, c_new, c_past, cache_kv, layer):
    n_seq, n_pages = page_table.shape
    n_new = qf.shape[1]
    r = n_new * N_HEADS
    depth, n_pool = cache_kv.shape[:2]
    out = _paged_call(
        functools.partial(_fox_sample_kernel, layer=layer, n_pages=n_pages, n_new=n_new), n_seq, 1,
        [_seq_block((r, LANES)), _seq_block((r, 1)), _seq_block((N_HEADS, n_pages * PAGE)),
         _seq_block((N_HEADS, LANES)), _seq_block((LANES, 2 * LANES)), _ANY],
        _seq_block((r, HEAD_DIM)), jax.ShapeDtypeStruct((n_seq, r, HEAD_DIM), BF16),
        [pltpu.VMEM((2, n_pages * PAGE, 2 * LANES), F32), pltpu.SemaphoreType.DMA((2,))],
        "fox_sample",
        (page_table, qf.reshape(n_seq, r, LANES), c_new.reshape(n_seq, r, 1), jnp.transpose(c_past, (0, 2, 1)),
         jnp.transpose(_pad_keys(c_new), (0, 2, 1)), _pad_keys(rows[:, :, R_FOX:R_FOX + 2 * LANES]),
         cache_kv.reshape(depth, n_pool, PAGE, 2 * LANES)))
    return out.reshape(n_seq, n_new, N_HEADS * HEAD_DIM)


def _mla_sample_kernel(pt_ref, q_ref, lat_ref, g_ref, wuk_ref, wuv_ref, cache_ref, o_ref, buf_ref, sem_ref,
                       stage_ref, *, layer, n_pages, n_new):
    slot = _paged_pipeline(pt_ref, n_pages, lambda page: cache_ref.at[layer, page], buf_ref, sem_ref, PAGE)
    q = q_ref[0]
    rows = q.shape[0]
    qa, qr = q[:, :LANES], q[:, LANES:]
    wuk = wuk_ref[...]
    scale = MLA_QK_DIM ** -0.5

    def key_norm(lat):
        kt = _dot_nt(wuk, lat)
        ssq = jnp.sum((kt * kt).reshape(N_HEADS, MLA_NOPE_DIM, lat.shape[0]), axis=1)
        return lax.rsqrt(ssq * (1.0 / MLA_NOPE_DIM) + EPS)

    def chunk(c, carry):
        sl = pl.ds(pl.multiple_of(c * KEY_CHUNK, KEY_CHUNK), KEY_CHUNK)
        row = buf_ref[slot, sl, :]
        lat = row[:, :MLA_KV_RANK].astype(BF16)
        kr = row[:, MLA_KV_RANK:].astype(BF16)
        s = (_rows3(_dot_nt(qa, lat)) * key_norm(lat)[None] + _rows3(_dot_nt(qr[:, :MLA_ROPE_DIM], kr))) * scale
        return _online(s.reshape(rows, KEY_CHUNK), lat, *carry)

    init = (jnp.full((rows, 1), NEG, F32), jnp.zeros((rows, 1), F32), jnp.zeros((rows, LANES), F32))
    carry = lax.fori_loop(0, n_pages * PAGE // KEY_CHUNK, chunk, init)
    lat = lat_ref[0].astype(BF16)
    s = (_rows3(_dot_nt(qa, lat)) * key_norm(lat)[None] + _rows3(_dot_nt(qr, g_ref[0].astype(BF16)))) * scale
    s = jnp.where(_new_key_mask(s, n_new), s, NEG)
    m, l, acc = _online(s.reshape(rows, LANES), lat, *carry)
    _mla_finish(acc, l, wuv_ref, o_ref, stage_ref)


def _mla_sample(page_table, qm, rows, wuk_t, wuv, cache, layer):
    n_seq, n_pages = page_table.shape
    n_new = qm.shape[1]
    r = n_new * N_HEADS
    width = MLA_KV_RANK + MLA_ROPE_DIM
    return _paged_call(
        functools.partial(_mla_sample_kernel, layer=layer, n_pages=n_pages, n_new=n_new), n_seq, 1,
        [_seq_block((r, 2 * LANES)), _seq_block((LANES, LANES)), _seq_block((LANES, LANES)),
         _const_block((N_HEADS * MLA_NOPE_DIM, MLA_KV_RANK)), _const_block((N_HEADS, MLA_KV_RANK, MLA_V_DIM)), _ANY],
        _seq_block((n_new, N_HEADS * MLA_V_DIM)), jax.ShapeDtypeStruct((n_seq, n_new, N_HEADS * MLA_V_DIM), BF16),
        [pltpu.VMEM((2, n_pages * PAGE, width), F32), pltpu.SemaphoreType.DMA((2,)), pltpu.VMEM((r, LANES), F32)],
        "mla_sample",
        (page_table, qm.reshape(n_seq, r, 2 * LANES), _pad_keys(rows[:, :, R_LAT:R_LAT + LANES]),
         _pad_keys(rows[:, :, R_G:R_G + LANES]), wuk_t, wuv, cache))


def _compress_sample_kernel(pt_ref, phi_ref, cpos_ref, kn_ref, seg_ref, cache_ref, o_ref, buf_ref, sem_ref,
                            *, layer, n_pages):
    slot = _paged_pipeline(pt_ref, n_pages, lambda page: cache_ref.at[layer, page, :, pl.ds(0, LANES)],
                           buf_ref, sem_ref, PAGE)
    n_blocks = n_pages * PAGE // NSA_BLOCK
    o_ref[0] = jnp.zeros(o_ref.shape[1:], F32)
    o_ref[0, 0:n_blocks, :] = _compress_body(buf_ref.at[slot], phi_ref, cpos_ref, kn_ref, seg_ref, n_blocks)


def _compress_sample(page_table, cp, cache4, layer, nbp):
    n_seq, n_pages = page_table.shape
    return _paged_call(
        functools.partial(_compress_sample_kernel, layer=layer, n_pages=n_pages), n_seq, 1,
        [_const_block((NSA_BLOCK, LANES, LANES)), _const_block((NSA_BLOCK, LANES)), _const_block((1, LANES)),
         _const_block((LANES, LANES)), _ANY],
        _seq_block((nbp, LANES)), jax.ShapeDtypeStruct((n_seq, nbp, LANES), F32),
        [pltpu.VMEM((2, n_pages * PAGE, LANES), F32), pltpu.SemaphoreType.DMA((2,))],
        "nsa_compress_sample", (page_table, cp['phi'], cp['cpos'], cp['kn'], cp['seg'], cache4))


def _nsa_sample_kernel(pt_ref, idx_ref, q_ref, slc_ref, win_ref, state_ref, near_ref, bnew_ref, bwin_ref, cf_ref,
                       oc_ref, g_ref, cache_ref, o_ref, buf_ref, sem_ref, *, layer, n_new, n_blocks):
    b = pl.program_id(0)
    slot = b % 2
    seg_rows = NSA_TOPK * NSA_BLOCK

    def copies(seq, to_slot, wait):
        def body(e, carry):
            blk = idx_ref[seq, e]
            blk = jnp.where(blk == n_blocks - 1, 0, blk)
            page = pt_ref[seq, blk // (PAGE // NSA_BLOCK)]
            src = cache_ref.at[layer, page, pl.ds((blk % (PAGE // NSA_BLOCK)) * NSA_BLOCK, NSA_BLOCK),
                               pl.ds(LANES, LANES)]
            cp = pltpu.make_async_copy(
                src, buf_ref.at[to_slot, pl.ds(pl.multiple_of(e * NSA_BLOCK, NSA_BLOCK), NSA_BLOCK)],
                sem_ref.at[to_slot])
            if wait:
                cp.wait()
            else:
                cp.start()
            return carry
        lax.fori_loop(0, n_new * NSA_TOPK, body, 0)

    @pl.when(b == 0)
    def _():
        copies(0, 0, False)

    @pl.when(b + 1 < pl.num_programs(0))
    def _():
        copies(b + 1, 1 - slot, False)

    copies(b, slot, True)

    q = q_ref[0]
    cf = cf_ref[...]
    lane = _lane((N_HEADS, LANES))
    new_kv = slc_ref[0]
    bnew = bnew_ref[...]
    kvs, s_rows = [], []
    for t in range(n_new):
        r0 = t * N_HEADS
        kvb = buf_ref[slot, pl.ds(t * seg_rows, seg_rows), :].astype(BF16)
        kvs.append(kvb)
        pieces = []
        for kk in range(NSA_TOPK // 2):
            halves = []
            for e in (2 * kk, 2 * kk + 1):
                blk = idx_ref[b, t * NSA_TOPK + e]
                bias = jnp.where(blk == n_blocks - 2, near_ref[1, r0:r0 + N_HEADS, :],
                                 jnp.where(blk == n_blocks - 3, near_ref[0, r0:r0 + N_HEADS, :],
                                           jnp.broadcast_to(cf[r0:r0 + N_HEADS, :], (N_HEADS, LANES))))
                halves.append(jnp.where(blk == n_blocks - 1, NEG, bias))
            pieces.append(jnp.where(lane < NSA_BLOCK, halves[0], halves[1]))
        s_all = _dot_nt(q, kvb[:, :HEAD_DIM])
        s_rows.append(s_all[r0:r0 + N_HEADS, :] + jnp.concatenate(pieces, axis=1))
    s1 = jnp.concatenate(s_rows, axis=0)
    new_b = new_kv.astype(BF16)
    s2 = _dot_nt(q, new_b[:, :HEAD_DIM]) + bnew
    m = jnp.maximum(jnp.max(s1, axis=-1, keepdims=True), jnp.max(s2, axis=-1, keepdims=True))
    p1, p2 = jnp.exp(s1 - m), jnp.exp(s2 - m)
    l = jnp.sum(p1, axis=-1, keepdims=True) + jnp.sum(p2, axis=-1, keepdims=True)
    token = lax.broadcasted_iota(I32, p1.shape, 0) // N_HEADS
    o = _dot(p2.astype(BF16), new_b)
    for t in range(n_new):
        o = o + _dot(jnp.where(token == t, p1, 0.0).astype(BF16), kvs[t])
    o_s = o[:, HEAD_DIM:] / jnp.maximum(l, 1e-30)
    st = state_ref[0, 0].astype(BF16)
    nw = win_ref[0].astype(BF16)
    s1 = _dot_nt(q, st[:, :HEAD_DIM]) + bwin_ref[...]
    s2 = _dot_nt(q, nw[:, :HEAD_DIM]) + bnew
    m = jnp.maximum(jnp.max(s1, axis=-1, keepdims=True), jnp.max(s2, axis=-1, keepdims=True))
    p1, p2 = jnp.exp(s1 - m), jnp.exp(s2 - m)
    l = jnp.sum(p1, axis=-1, keepdims=True) + jnp.sum(p2, axis=-1, keepdims=True)
    o_w = (_dot(p1.astype(BF16), st) + _dot(p2.astype(BF16), nw))[:, HEAD_DIM:] / jnp.maximum(l, 1e-30)
    g = g_ref[0]
    o_ref[0] = (g[:, 0:1] * oc_ref[0] + g[:, 1:2] * o_s + g[:, 2:3] * o_w).astype(BF16)


def _nsa_sample_branch(page_table, qn, rows, cp, tbl, cache4, win_state4, layer):
    n_seq, n_pages = page_table.shape
    n_new = qn.shape[1]
    past = n_pages * PAGE
    n_blocks = past // NSA_BLOCK + 1
    win = win_state4.shape[2]
    assert n_new <= NSA_BLOCK and n_blocks > NSA_TOPK + 3 and win == NSA_WINDOW
    nbp = -(-n_blocks // LANES) * LANES
    r = n_new * N_HEADS
    qpos = past + jnp.arange(n_new, dtype=I32)
    kvc = _compress_sample(page_table, cp, cache4, layer, nbp)
    cend = jnp.arange(nbp, dtype=I32) * NSA_BLOCK + (NSA_BLOCK - 1)
    dist_c = qpos[:, None] - cend[None, :]
    bias_c = _bias_lookup(tbl, dist_c, dist_c >= 0)
    q_rows = qn.reshape(n_seq, r, HEAD_DIM)
    oc, imp = _cmp_attn(q_rows, kvc, bias_c, _forced_code(qpos, nbp), n_new)
    _, idx = _topk(imp.reshape(n_seq * n_new, nbp))
    idx = idx.reshape(n_seq, n_new * NSA_TOPK)
    u = jnp.arange(LANES, dtype=I32)
    d_near = qpos[:, None] - (past - LANES + u)[None, :]
    near = _bias_lookup(tbl, d_near, d_near >= 0)
    near = jnp.stack([jnp.tile(near[:, :NSA_BLOCK], (1, 2)), jnp.tile(near[:, NSA_BLOCK:], (1, 2))])
    d_new = jnp.arange(n_new, dtype=I32)[:, None] - u[None, :]
    bnew = _bias_lookup(tbl, d_new, (d_new >= 0) & (u < n_new)[None, :])
    d_win = qpos[:, None] - (past - win + jnp.arange(win, dtype=I32))[None, :]
    bwin = _bias_lookup(tbl, d_win, d_win <= NSA_WINDOW)
    cf = jnp.tile(tbl[REL_MAX_DIST], n_new)[:, None]
    g = rows[:, :, R_G + G_GATE:R_G + G_GATE + 3 * N_HEADS].reshape(n_seq, n_new, 3, N_HEADS)
    g = jnp.transpose(g, (0, 1, 3, 2)).reshape(n_seq, r, 3)
    out = _paged_call(
        functools.partial(_nsa_sample_kernel, layer=layer, n_new=n_new, n_blocks=n_blocks), n_seq, 2,
        [_seq_block((r, HEAD_DIM)), _seq_block((LANES, LANES)), _seq_block((LANES, LANES)),
         pl.BlockSpec((1, 1, win, LANES), lambda b, *_: (layer, b, 0, 0)),
         _const_block((2, r, LANES)), _const_block((r, LANES)), _const_block((r, win)), _const_block((r, 1)),
         _seq_block((r, HEAD_DIM)), _seq_block((r, 3)), _ANY],
        _seq_block((r, HEAD_DIM)), jax.ShapeDtypeStruct((n_seq, r, HEAD_DIM), BF16),
        [pltpu.VMEM((2, n_new * NSA_TOPK * NSA_BLOCK, LANES), F32), pltpu.SemaphoreType.DMA((2,))],
        "nsa_sample",
        (page_table, idx, q_rows, _pad_keys(rows[:, :, R_NSA + LANES:R_NSA + 2 * LANES]),
         _pad_keys(rows[:, :, R_WIN:R_WIN + LANES]), win_state4, near, bnew, bwin, cf, oc, g, cache4))
    return out.reshape(n_seq, n_new, N_HEADS * HEAD_DIM)


def _cache_rows(rows):
    n, t, _ = rows.shape
    fox_kv = rows[:, :, R_FOX:R_FOX + 2 * LANES].reshape(n, t, 2, FOX_KV_HEADS, HEAD_DIM)
    logf = rows[:, :, R_G + G_LOGF:R_G + G_LOGF + N_HEADS]
    nsa_kv = rows[:, :, R_NSA:R_NSA + 2 * LANES].reshape(n, t, 4, 1, HEAD_DIM)
    mla = rows[:, :, R_LAT:R_LAT + MLA_KV_RANK + MLA_ROPE_DIM]
    win = rows[:, :, R_WIN:R_WIN + LANES].reshape(n, t, 2, 1, HEAD_DIM)
    return fox_kv, logf, nsa_kv, mla, win


def _channel(x, fox, nsa, mla, p, fw, seq_len, prev=None):
    t = x.shape[0]
    h = _attn_out(x, fox.reshape(t, -1), nsa.reshape(t, -1), mla.reshape(t, -1), fw['w_out'])
    h, conv = _ffn(h, fw, seq_len=seq_len, prev=prev)
    return _ple(h, p.reshape(t, -1), fw), conv


def kernel(x_prompt, x_sample, cache_fox_kv, cache_fox_logf, cache_nsa_kv, cache_mla, state_nsa_win, state_ffn_conv, page_table, p_prompt, p_sample, rel_bias, norm_mix, w_in, fox_qn, fox_kn, fox_fb, nsa_qn, nsa_kn, nsa_phi, nsa_cpos, mla_cqn, mla_wuq, mla_qn, mla_ckvn, mla_wukv, mla_kn, w_out, norm_ffn, ffn_up, ffn_conv, ffn_conv_b, ffn_down, ple_norm, ple_gate, ple_proj):
    layer_params = dict(norm_mix=norm_mix, w_in=w_in, fox_qn=fox_qn, fox_kn=fox_kn, fox_fb=fox_fb, nsa_qn=nsa_qn,
                        nsa_kn=nsa_kn, nsa_phi=nsa_phi, nsa_cpos=nsa_cpos, mla_cqn=mla_cqn, mla_wuq=mla_wuq,
                        mla_qn=mla_qn, mla_ckvn=mla_ckvn, mla_wukv=mla_wukv, mla_kn=mla_kn, w_out=w_out,
                        norm_ffn=norm_ffn, ffn_up=ffn_up, ffn_conv=ffn_conv, ffn_conv_b=ffn_conv_b,
                        ffn_down=ffn_down, ple_norm=ple_norm, ple_gate=ple_gate, ple_proj=ple_proj)
    depth = w_in.shape[0]
    b, s, _ = x_prompt.shape
    n, t_new, _ = x_sample.shape
    n_pool = cache_fox_kv.shape[1]
    assert cache_fox_kv.shape[2] == PAGE
    past = page_table.shape[1] * PAGE
    win_buf = state_nsa_win.shape[2]
    win_keep = min(NSA_WINDOW, s)
    cache_nsa4 = cache_nsa_kv.reshape(depth, n_pool, PAGE, 2 * LANES)
    win_state4 = state_nsa_win.reshape(depth, n, win_buf, LANES)
    tbl = _bias_by_dist(rel_bias)
    rope_p = jnp.tile(_rope_tables(jnp.arange(s, dtype=I32)), (b, 1))
    rope_s = jnp.tile(_rope_tables(past + jnp.arange(t_new, dtype=I32)), (n, 1))
    hp = x_prompt.reshape(b * s, D_MODEL)
    hs = x_sample.reshape(n * t_new, D_MODEL)
    outs = [[] for _ in range(12)]
    for l in range(depth):
        lw = {k: v[l] for k, v in layer_params.items()}
        pp = _layer_proj_params(lw)
        cp = _compress_params(lw)
        fw = _channel_params(lw)
        qf, qn, qm, rows2 = _project(hp, rope_p, pp)
        rows = rows2.reshape(b, s, ROWS_W)
        fox_kv, logf, nsa_kv, mla_rows, win_rows = _cache_rows(rows)
        fox = _fox_prompt(qf.reshape(b, s, -1), rows, _scan(logf, s))
        mla = _mla_prompt(qm.reshape(b, s, -1), rows, _mla_rs(rows2, pp['wuk_t']).reshape(b, s, N_HEADS), pp['wuv'])
        nsa = _nsa_prompt_branch(qn.reshape(b, s, -1), rows, cp, tbl)
        hp, tail = _channel(hp, fox, nsa, mla, p_prompt[l], fw, s)
        conv_p = tail.reshape(b, -1, SUBLANES, D_FF)[:, -1, SUBLANES - (CONV_W - 1):]
        for o, v in zip(outs[:6], (fox_kv, logf, nsa_kv, mla_rows, win_rows[:, s - win_keep:], conv_p)):
            o.append(v)
        qf, qn, qm, rows2 = _project(hs, rope_s, pp)
        rows = rows2.reshape(n, t_new, ROWS_W)
        fox_kv, logf, nsa_kv, mla_rows, win_rows = _cache_rows(rows)
        c_new = _scan(logf.reshape(1, n * t_new, N_HEADS), t_new).reshape(n, t_new, N_HEADS)
        c_past = _scan_paged(page_table, cache_fox_logf, l)
        fox = _fox_sample(page_table, qf.reshape(n, t_new, -1), rows, c_new, c_past, cache_fox_kv, l)
        mla = _mla_sample(page_table, qm.reshape(n, t_new, -1), rows, pp['wuk_t'], pp['wuv'], cache_mla, l)
        nsa = _nsa_sample_branch(page_table, qn.reshape(n, t_new, -1), rows, cp, tbl, cache_nsa4, win_state4, l)
        st = state_ffn_conv[l]
        zero = jnp.zeros((n, t_new - 1, D_FF), F32)
        prev1 = jnp.concatenate([st[:, 1:2], zero], axis=1).reshape(n * t_new, D_FF)
        prev2 = jnp.concatenate([st[:, 0:2], zero[:, 1:]], axis=1).reshape(n * t_new, D_FF)
        hs, a = _channel(hs, fox, nsa, mla, p_sample[l], fw, t_new, prev=(prev1, prev2))
        win_s = jnp.concatenate([state_nsa_win[l], win_rows], axis=1)[:, -win_buf:]
        conv_s = jnp.concatenate([st, a.reshape(n, t_new, D_FF)], axis=1)[:, -(CONV_W - 1):]
        for o, v in zip(outs[6:], (fox_kv, logf, nsa_kv, mla_rows, win_s, conv_s)):
            o.append(v)
    return (hp.reshape(b, s, D_MODEL), hs.reshape(n, t_new, D_MODEL)) + tuple(jnp.stack(o) for o in outs)
```

```python
import functools
import math

import numpy as np
import jax
import jax.numpy as jnp
from jax import lax
from jax.experimental import pallas as pl
from jax.experimental.pallas import tpu as pltpu

F32, BF16, I32 = jnp.float32, jnp.bfloat16, jnp.int32

D_MODEL = 2048
HEAD_DIM = 64
N_HEADS = 8
FOX_KV_HEADS = 2
FOX_GROUP = N_HEADS // FOX_KV_HEADS
NSA_BLOCK = 64
NSA_TOPK = 16
NSA_WINDOW = 512
NSA_FORCED_SCORE = 1e3
MLA_V_DIM = 128
MLA_Q_RANK = 384
MLA_KV_RANK = 128
MLA_NOPE_DIM = 64
MLA_ROPE_DIM = 32
MLA_QK_DIM = MLA_NOPE_DIM + MLA_ROPE_DIM
ROPE_BASE = 10000.0
REL_BUCKETS = 32
REL_MAX_DIST = 128
D_FF = 5632
CONV_W = 3
EPS = 1e-6
NEG = -1e30

LANES = 128
SUBLANES = 8
VMEM_LIMIT = 56 * 1024 * 1024

Q_TILE = 128
ROWS_W = 7 * LANES
R_FOX, R_NSA, R_WIN, R_LAT, R_G = 0, 256, 512, 640, 768
G_KR, G_LOGF, G_GATE = 0, 32, 40
Z_FQ, Z_NQ, Z_FK, Z_FV, Z_NKV, Z_CKV, Z_CQ, Z_G, Z_END = 0, 1024, 1536, 1664, 1792, 2176, 2304, 2688, 2816


def _cparams(sem, vmem=VMEM_LIMIT):
    return pltpu.CompilerParams(dimension_semantics=sem, vmem_limit_bytes=vmem)


def _dot(a, b):
    return jnp.dot(a, b, preferred_element_type=F32)


def _dot_nt(a, b):
    return lax.dot_general(a, b, (((1,), (1,)), ((), ())), preferred_element_type=F32)


def _seg_mean(x, seg):
    hi = x.astype(BF16)
    lo = (x - hi.astype(F32)).astype(BF16)
    return _dot(hi, seg) + _dot(lo, seg)


def _lane(shape):
    return lax.broadcasted_iota(I32, shape, len(shape) - 1)


def _proj_kernel(x_ref, gmix_ref, w_ref, gain_ref, seg_ref, wuq_ref, gmla_ref, wuk_ref, rope_ref,
                 qf_ref, qn_ref, qm_ref, rows_ref):
    x = x_ref[...]
    xn = (x * lax.rsqrt(jnp.mean(x * x, axis=-1, keepdims=True) + EPS) * gmix_ref[...]).astype(BF16)
    z = _dot(xn, w_ref[...])
    gain = gain_ref[...]
    seg_pair, seg_first, seg_kr, seg_mla = seg_ref[0], seg_ref[1], seg_ref[2], seg_ref[3]
    lane = _lane((x.shape[0], LANES))

    def head_norm(lo, seg):
        zs = z[:, lo:lo + LANES]
        return zs * lax.rsqrt(_seg_mean(zs * zs, seg) + EPS) * gain[:, lo:lo + LANES]

    for i in range(8):
        qf_ref[:, i * LANES:(i + 1) * LANES] = head_norm(Z_FQ + i * LANES, seg_pair).astype(BF16)
    for i in range(4):
        qn_ref[:, i * LANES:(i + 1) * LANES] = head_norm(Z_NQ + i * LANES, seg_pair).astype(BF16)
    rows_ref[:, R_FOX:R_FOX + LANES] = head_norm(Z_FK, seg_pair)
    rows_ref[:, R_FOX + LANES:R_FOX + 2 * LANES] = z[:, Z_FV:Z_FV + LANES]
    rows_ref[:, R_NSA:R_NSA + LANES] = z[:, Z_NKV:Z_NKV + LANES]
    for i in (1, 2):
        lo = Z_NKV + i * LANES
        rows_ref[:, R_NSA + i * LANES:R_NSA + (i + 1) * LANES] = jnp.where(
            lane < HEAD_DIM, head_norm(lo, seg_first), z[:, lo:lo + LANES])
    ckv = z[:, Z_CKV:Z_CKV + LANES]
    rows_ref[:, R_LAT:R_LAT + LANES] = (ckv * lax.rsqrt(jnp.mean(ckv * ckv, axis=-1, keepdims=True) + EPS)
                                        * gain[:, Z_CKV:Z_CKV + LANES])
    zg = z[:, Z_G:Z_G + LANES]
    gg = gain[:, Z_G:Z_G + LANES]
    cg, sg = rope_ref[:, 2 * LANES:3 * LANES], rope_ref[:, 3 * LANES:4 * LANES]
    zk = jnp.where(lane < MLA_ROPE_DIM, zg, 0.0)
    kr = zk * lax.rsqrt(_seg_mean(zk * zk, seg_kr) + EPS) * gg
    half = MLA_ROPE_DIM // 2
    kr = kr * cg + jnp.where(lane < half, pltpu.roll(kr, LANES - half, 1), pltpu.roll(kr, half, 1)) * sg
    a = zg + gg
    logf = jnp.minimum(a, 0.0) - jnp.log(1.0 + jnp.exp(-jnp.abs(a)))
    gate = 1.0 / (1.0 + jnp.exp(-zg))
    rows_ref[:, R_G:R_G + LANES] = jnp.where(
        lane < G_LOGF, kr, jnp.where(lane < G_GATE, logf, jnp.where(lane < G_GATE + 3 * N_HEADS, gate, 0.0)))
    cq = z[:, Z_CQ:Z_CQ + MLA_Q_RANK]
    cqn = (cq * lax.rsqrt(jnp.mean(cq * cq, axis=-1, keepdims=True) + EPS)
           * gain[:, Z_CQ:Z_CQ + MLA_Q_RANK]).astype(BF16)
    qh = _dot(cqn, wuq_ref[...])
    cq_t, sq_t = rope_ref[:, 0:LANES], rope_ref[:, LANES:2 * LANES]
    gmla = gmla_ref[...]
    x1_hi = MLA_NOPE_DIM + half
    for h in range(N_HEADS):
        qs = qh[:, h * LANES:(h + 1) * LANES]
        qs = qs * lax.rsqrt(_seg_mean(qs * qs, seg_mla) + EPS) * gmla
        qs = qs * cq_t + jnp.where(lane < x1_hi, pltpu.roll(qs, LANES - half, 1), pltpu.roll(qs, half, 1)) * sq_t
        qa = _dot(qs.astype(BF16), wuk_ref[h])
        qr = jnp.where(lane < MLA_ROPE_DIM, pltpu.roll(qs, LANES - MLA_NOPE_DIM, 1), 0.0)
        qm_ref[:, 2 * h * LANES:(2 * h + 1) * LANES] = qa.astype(BF16)
        qm_ref[:, (2 * h + 1) * LANES:(2 * h + 2) * LANES] = qr.astype(BF16)


def _seg_matrix(blocks):
    m = np.zeros((LANES, LANES), np.float32)
    for lo, n in blocks:
        m[lo:lo + n, lo:lo + n] = 1.0 / n
    return m


_SEG = np.stack([
    _seg_matrix([(0, 64), (64, 64)]),
    _seg_matrix([(0, 64)]),
    _seg_matrix([(0, MLA_ROPE_DIM)]),
    _seg_matrix([(0, MLA_NOPE_DIM), (MLA_NOPE_DIM, MLA_ROPE_DIM)]),
])


def _layer_proj_params(lw):
    w = lw['w_in']
    o = np.cumsum([0, 512, 128, 128, 8, 512, 384, 24, 384, 128, 32])
    fq, fk, fv, ff, nq, nkv, ng, mcq, mckv, mkr = [w[:, o[i]:o[i + 1]] for i in range(10)]
    zeros = lambda n: jnp.zeros((D_MODEL, n), F32)
    fq_cols = []
    for h in range(N_HEADS):
        part = fq[:, h * HEAD_DIM:(h + 1) * HEAD_DIM]
        fq_cols += [part, zeros(HEAD_DIM)] if h < FOX_GROUP else [zeros(HEAD_DIM), part]
    wp = jnp.concatenate(fq_cols + [nq, fk, fv, nkv, mckv, mcq, mkr, ff, ng, zeros(LANES - 64)], axis=1)
    one = lambda n: jnp.ones((n,), F32)
    zero = lambda n: jnp.zeros((n,), F32)
    scale = HEAD_DIM ** -0.5
    fq_gain = []
    for h in range(N_HEADS):
        fq_gain += [lw['fox_qn'] * scale, zero(HEAD_DIM)] if h < FOX_GROUP else [zero(HEAD_DIM), lw['fox_qn'] * scale]
    gain = jnp.concatenate(fq_gain + [jnp.tile(lw['nsa_qn'] * scale, N_HEADS), jnp.tile(lw['fox_kn'], 2), one(LANES),
                                      one(LANES), lw['nsa_kn'][1], one(64), lw['nsa_kn'][2], one(64),
                                      lw['mla_ckvn'], lw['mla_cqn'], lw['mla_kn'][MLA_NOPE_DIM:], lw['fox_fb'],
                                      zero(LANES - 40)])[None, :]
    wuq = lw['mla_wuq'].reshape(MLA_Q_RANK, N_HEADS, MLA_QK_DIM)
    wuq = jnp.pad(wuq, ((0, 0), (0, 0), (0, LANES - MLA_QK_DIM))).reshape(MLA_Q_RANK, N_HEADS * LANES)
    gmla = jnp.concatenate([lw['mla_qn'], zero(LANES - MLA_QK_DIM)])[None, :]
    wukv = lw['mla_wukv'].reshape(MLA_KV_RANK, N_HEADS, MLA_NOPE_DIM + MLA_V_DIM)
    wuk = wukv[:, :, :MLA_NOPE_DIM]
    wuv = wukv[:, :, MLA_NOPE_DIM:]
    kgain = lw['mla_kn'][:MLA_NOPE_DIM]
    wuk_abs = jnp.transpose(wuk, (1, 2, 0)) * kgain[None, :, None]
    wuk_abs = jnp.pad(wuk_abs, ((0, 0), (0, LANES - MLA_NOPE_DIM), (0, 0)))
    return dict(w=wp.astype(BF16), gain=gain, gmix=lw['norm_mix'][None, :], wuq=wuq.astype(BF16), gmla=gmla,
                wuk_abs=wuk_abs.astype(BF16),
                wuk_t=jnp.transpose(wuk, (1, 2, 0)).reshape(N_HEADS * MLA_NOPE_DIM, MLA_KV_RANK).astype(BF16),
                wuv=jnp.transpose(wuv, (1, 0, 2)).astype(BF16))


def _rope_tables(pos):
    inv = ROPE_BASE ** (-jnp.arange(0, MLA_ROPE_DIM, 2, dtype=F32) / MLA_ROPE_DIM)
    ang = pos.astype(F32)[:, None] * inv
    cos, sin = jnp.cos(ang), jnp.sin(ang)
    n = pos.shape[0]
    one = lambda k: jnp.ones((n, k), F32)
    zero = lambda k: jnp.zeros((n, k), F32)
    half = MLA_ROPE_DIM // 2
    rest = LANES - MLA_NOPE_DIM - MLA_ROPE_DIM
    cq = jnp.concatenate([one(MLA_NOPE_DIM), cos, cos, one(rest)], axis=1)
    sq = jnp.concatenate([zero(MLA_NOPE_DIM), -sin, sin, zero(rest)], axis=1)
    cg = jnp.concatenate([cos, cos, one(LANES - 2 * half)], axis=1)
    sg = jnp.concatenate([-sin, sin, zero(LANES - 2 * half)], axis=1)
    return jnp.concatenate([cq, sq, cg, sg], axis=1)


def _project(x, rope, pp, tm=256):
    t = x.shape[0]
    tm = min(tm, t)
    assert t % tm == 0
    const = lambda shape: pl.BlockSpec(shape, lambda i: (0,) * len(shape))
    tok = lambda w: pl.BlockSpec((tm, w), lambda i: (i, 0))
    return pl.pallas_call(
        _proj_kernel,
        grid=(t // tm,),
        in_specs=[tok(D_MODEL), const((1, D_MODEL)), const((D_MODEL, Z_END)), const((1, Z_END)),
                  const((4, LANES, LANES)), const((MLA_Q_RANK, N_HEADS * LANES)), const((1, LANES)),
                  const((N_HEADS, LANES, LANES)), tok(4 * LANES)],
        out_specs=[tok(N_HEADS * LANES), tok(N_HEADS * HEAD_DIM), tok(2 * N_HEADS * LANES), tok(ROWS_W)],
        out_shape=[jax.ShapeDtypeStruct((t, N_HEADS * LANES), BF16),
                   jax.ShapeDtypeStruct((t, N_HEADS * HEAD_DIM), BF16),
                   jax.ShapeDtypeStruct((t, 2 * N_HEADS * LANES), BF16),
                   jax.ShapeDtypeStruct((t, ROWS_W), F32)],
        compiler_params=_cparams(("parallel",)),
        name="proj",
    )(x, pp['gmix'], pp['w'], pp['gain'], jnp.asarray(_SEG, BF16), pp['wuq'], pp['gmla'], pp['wuk_abs'], rope)


POS_PER_ROW = LANES // N_HEADS


def _scan_body(x, buf_ref, seg_len, reverse):
    r = x.shape[0]
    lane = _lane(x.shape)
    pos = lax.broadcasted_iota(I32, x.shape, 0) * POS_PER_ROW + lane // N_HEADS
    segpos = pos % seg_len
    buf_ref[...] = jnp.zeros(buf_ref.shape, F32)

    def row_shift(a, k):
        if k == 0:
            return a
        buf_ref[pl.ds(r, r), :] = a
        return buf_ref[pl.ds(r - k, r), :]

    def shift(y, sh):
        lane_sh, row_sh = (sh % POS_PER_ROW) * N_HEADS, sh // POS_PER_ROW
        sign = -1 if reverse else 1
        if lane_sh == 0:
            return row_shift(y, sign * row_sh)
        a = pltpu.roll(y, (LANES - lane_sh) if reverse else lane_sh, 1)
        same = (lane < LANES - lane_sh) if reverse else (lane >= lane_sh)
        return jnp.where(same, row_shift(a, sign * row_sh), row_shift(a, sign * (row_sh + 1)))

    y = x
    sh = 1
    while sh < seg_len and sh // POS_PER_ROW < r:
        valid = (segpos + sh < seg_len) if reverse else (segpos >= sh)
        y = y + jnp.where(valid, shift(y, sh), 0.0)
        sh *= 2
    return (x - y) if reverse else y


def _scan_kernel(x_ref, o_ref, buf_ref, *, seg_len, reverse):
    o_ref[0] = _scan_body(x_ref[0], buf_ref, seg_len, reverse)


def _scan(x, seg_len, reverse=False):
    n, p_in, h = x.shape
    assert h == N_HEADS
    p = -(-p_in // (POS_PER_ROW * SUBLANES)) * (POS_PER_ROW * SUBLANES)
    x = jnp.pad(x, ((0, 0), (0, p - p_in), (0, 0)))
    r = p // POS_PER_ROW
    out = pl.pallas_call(
        functools.partial(_scan_kernel, seg_len=seg_len, reverse=reverse),
        grid=(n,),
        in_specs=[pl.BlockSpec((1, r, LANES), lambda i: (i, 0, 0))],
        out_specs=pl.BlockSpec((1, r, LANES), lambda i: (i, 0, 0)),
        out_shape=jax.ShapeDtypeStruct((n, r, LANES), F32),
        scratch_shapes=[pltpu.VMEM((3 * r, LANES), F32)],
        compiler_params=_cparams(("parallel",)),
        name="logf_scan",
    )(x.reshape(n, r, LANES))
    return out.reshape(n, p, h)[:, :p_in]


def _online(s, v, m, l, acc, v_t=False):
    m_new = jnp.maximum(m, jnp.max(s, axis=-1, keepdims=True))
    alpha = jnp.exp(m - m_new)
    p = jnp.exp(s - m_new)
    l = alpha * l + jnp.sum(p, axis=-1, keepdims=True)
    pv = _dot_nt(p.astype(BF16), v) if v_t else _dot(p.astype(BF16), v)
    return m_new, l, alpha * acc + pv


def _rows3(x):
    return x.reshape(x.shape[0] // N_HEADS, N_HEADS, x.shape[1])


def _fox_prompt_kernel(q_ref, c_ref, ct_ref, kv_ref, o_ref):
    i = pl.program_id(1)
    q = q_ref[0]
    rows = q.shape[0]
    cq = c_ref[0]

    def tile(j, carry, diag):
        m, l, acc = carry
        kv = kv_ref[0, pl.ds(pl.multiple_of(j * Q_TILE, Q_TILE), Q_TILE), :]
        k, v = kv[:, :LANES].astype(BF16), kv[:, LANES:].astype(BF16)
        ck = ct_ref[0, :, pl.ds(pl.multiple_of(j * Q_TILE, Q_TILE), Q_TILE)]
        s = _rows3(_dot_nt(q, k)) + (_rows3(jnp.broadcast_to(cq, (rows, Q_TILE))) - ck[None])
        if diag:
            tq = lax.broadcasted_iota(I32, s.shape, 0)
            tk = lax.broadcasted_iota(I32, s.shape, 2)
            s = jnp.where(tk <= tq, s, NEG)
        return _online(s.reshape(rows, Q_TILE), v, m, l, acc)

    init = (jnp.full((rows, 1), NEG, F32), jnp.zeros((rows, 1), F32), jnp.zeros((rows, LANES), F32))
    carry = lax.fori_loop(0, i, lambda j, c: tile(j, c, False), init)
    m, l, acc = tile(i, carry, True)
    o = acc / jnp.maximum(l, 1e-30)
    head = lax.broadcasted_iota(I32, (rows, HEAD_DIM), 0) % N_HEADS
    o_ref[0] = jnp.where(head < FOX_GROUP, o[:, :HEAD_DIM], o[:, HEAD_DIM:]).astype(BF16)


def _fox_prompt(qf, rows, c):
    b, s, _ = qf.shape
    nq = s // Q_TILE
    out = pl.pallas_call(
        _fox_prompt_kernel,
        grid=(b, nq),
        in_specs=[pl.BlockSpec((1, Q_TILE * N_HEADS, LANES), lambda bi, i: (bi, i, 0)),
                  pl.BlockSpec((1, Q_TILE * N_HEADS, 1), lambda bi, i: (bi, i, 0)),
                  pl.BlockSpec((1, N_HEADS, s), lambda bi, i: (bi, 0, 0)),
                  pl.BlockSpec((1, s, 2 * LANES), lambda bi, i: (bi, 0, R_FOX // (2 * LANES)))],
        out_specs=pl.BlockSpec((1, Q_TILE * N_HEADS, HEAD_DIM), lambda bi, i: (bi, i, 0)),
        out_shape=jax.ShapeDtypeStruct((b, s * N_HEADS, HEAD_DIM), BF16),
        compiler_params=_cparams(("parallel", "parallel")),
        name="fox_prompt",
    )(qf.reshape(b, s * N_HEADS, LANES), c.reshape(b, s * N_HEADS, 1), jnp.transpose(c, (0, 2, 1)), rows)
    return out.reshape(b, s, N_HEADS * HEAD_DIM)


def _mla_rs_kernel(lat_ref, wuk_ref, red_ref, o_ref):
    k = _dot(lat_ref[...].astype(BF16), wuk_ref[...])
    o_ref[...] = lax.rsqrt(_seg_mean(k * k, red_ref[...]) + EPS)


_HEAD_REDUCE = np.zeros((N_HEADS * MLA_NOPE_DIM, LANES), np.float32)
for _h in range(N_HEADS):
    _HEAD_REDUCE[_h * MLA_NOPE_DIM:(_h + 1) * MLA_NOPE_DIM, _h] = 1.0 / MLA_NOPE_DIM


def _mla_rs(rows2d, wuk_t, tm=512):
    t = rows2d.shape[0]
    tm = min(tm, t)
    out = pl.pallas_call(
        _mla_rs_kernel,
        grid=(t // tm,),
        in_specs=[pl.BlockSpec((tm, LANES), lambda i: (i, R_LAT // LANES)),
                  pl.BlockSpec((MLA_KV_RANK, N_HEADS * MLA_NOPE_DIM), lambda i: (0, 0)),
                  pl.BlockSpec((N_HEADS * MLA_NOPE_DIM, LANES), lambda i: (0, 0))],
        out_specs=pl.BlockSpec((tm, LANES), lambda i: (i, 0)),
        out_shape=jax.ShapeDtypeStruct((t, LANES), F32),
        compiler_params=_cparams(("parallel",)),
        name="mla_rs",
    )(rows2d, wuk_t.T, jnp.asarray(_HEAD_REDUCE, BF16))
    return out[:, :N_HEADS]


def _mla_finish(acc, l, wuv_ref, o_ref, stage_ref):
    stage_ref[...] = acc / jnp.maximum(l, 1e-30)
    n_tok = acc.shape[0] // N_HEADS
    for h in range(N_HEADS):
        oh = stage_ref[pl.ds(h, n_tok, stride=N_HEADS), :]
        o_ref[0, :, h * MLA_V_DIM:(h + 1) * MLA_V_DIM] = _dot(oh.astype(BF16), wuv_ref[h]).astype(o_ref.dtype)


def _mla_prompt_kernel(q_ref, rst_ref, lat_ref, g_ref, wuv_ref, o_ref, stage_ref):
    i = pl.program_id(1)
    q = q_ref[0]
    rows = q.shape[0]
    qa, qr = q[:, :LANES], q[:, LANES:]
    scale = MLA_QK_DIM ** -0.5

    def tile(j, carry, diag):
        m, l, acc = carry
        sl = pl.ds(pl.multiple_of(j * Q_TILE, Q_TILE), Q_TILE)
        lat = lat_ref[0, sl, :].astype(BF16)
        kr = g_ref[0, sl, :].astype(BF16)
        s = (_rows3(_dot_nt(qa, lat)) * rst_ref[0, :, sl][None] + _rows3(_dot_nt(qr, kr))) * scale
        if diag:
            tq = lax.broadcasted_iota(I32, s.shape, 0)
            tk = lax.broadcasted_iota(I32, s.shape, 2)
            s = jnp.where(tk <= tq, s, NEG)
        return _online(s.reshape(rows, Q_TILE), lat, m, l, acc)

    init = (jnp.full((rows, 1), NEG, F32), jnp.zeros((rows, 1), F32), jnp.zeros((rows, LANES), F32))
    carry = lax.fori_loop(0, i, lambda j, c: tile(j, c, False), init)
    m, l, acc = tile(i, carry, True)
    _mla_finish(acc, l, wuv_ref, o_ref, stage_ref)


def _mla_prompt(qm, rows, rs, wuv):
    b, s, _ = qm.shape
    nq = s // Q_TILE
    return pl.pallas_call(
        _mla_prompt_kernel,
        grid=(b, nq),
        in_specs=[pl.BlockSpec((1, Q_TILE * N_HEADS, 2 * LANES), lambda bi, i: (bi, i, 0)),
                  pl.BlockSpec((1, N_HEADS, s), lambda bi, i: (bi, 0, 0)),
                  pl.BlockSpec((1, s, LANES), lambda bi, i: (bi, 0, R_LAT // LANES)),
                  pl.BlockSpec((1, s, LANES), lambda bi, i: (bi, 0, R_G // LANES)),
                  pl.BlockSpec((N_HEADS, MLA_KV_RANK, MLA_V_DIM), lambda bi, i: (0, 0, 0))],
        out_specs=pl.BlockSpec((1, Q_TILE, N_HEADS * MLA_V_DIM), lambda bi, i: (bi, i, 0)),
        out_shape=jax.ShapeDtypeStruct((b, s, N_HEADS * MLA_V_DIM), BF16),
        scratch_shapes=[pltpu.VMEM((Q_TILE * N_HEADS, LANES), F32)],
        compiler_params=_cparams(("parallel", "parallel")),
        name="mla_prompt",
    )(qm.reshape(b, s * N_HEADS, 2 * LANES), jnp.transpose(rs, (0, 2, 1)), rows, rows, wuv)


def _compress_body(x_ref, phi_ref, cpos_ref, kn_ref, seg_ref, n_blocks):
    accs = [jnp.zeros((n_blocks, LANES), F32) for _ in range(4)]
    for j in range(NSA_BLOCK):
        xj = x_ref[pl.ds(j, n_blocks, stride=NSA_BLOCK), :] + cpos_ref[j:j + 1, :]
        accs[j % 4] = accs[j % 4] + _dot(xj.astype(BF16), phi_ref[j])
    kv = (accs[0] + accs[1]) + (accs[2] + accs[3])
    lane = _lane(kv.shape)
    kc = jnp.where(lane < HEAD_DIM, kv, 0.0)
    kc = kc * lax.rsqrt(_seg_mean(kc * kc, seg_ref[...]) + EPS) * kn_ref[...]
    return jnp.where(lane < HEAD_DIM, kc, kv)


def _compress_params(lw):
    phi = lw['nsa_phi']
    z = jnp.zeros_like(phi[0])
    phi_j = jnp.concatenate([jnp.concatenate([phi[0], z], axis=2), jnp.concatenate([z, phi[1]], axis=2)], axis=1)
    cpos = jnp.concatenate([lw['nsa_cpos'][0], lw['nsa_cpos'][1]], axis=1)
    kn = jnp.concatenate([lw['nsa_kn'][0], jnp.ones((HEAD_DIM,), F32)])[None, :]
    return dict(phi=phi_j.astype(BF16), cpos=cpos, kn=kn, seg=jnp.asarray(_SEG[1], BF16))


def _compress_prompt_kernel(x_ref, phi_ref, cpos_ref, kn_ref, seg_ref, o_ref, *, n_blocks):
    o_ref[0] = jnp.zeros(o_ref.shape[1:], F32)
    o_ref[0, 0:n_blocks, :] = _compress_body(x_ref.at[0], phi_ref, cpos_ref, kn_ref, seg_ref, n_blocks)


def _compress_prompt(rows, cp, nbp):
    b, s, _ = rows.shape
    n_blocks = s // NSA_BLOCK
    const = lambda shape: pl.BlockSpec(shape, lambda i: (0,) * len(shape))
    return pl.pallas_call(
        functools.partial(_compress_prompt_kernel, n_blocks=n_blocks),
        grid=(b,),
        in_specs=[pl.BlockSpec((1, s, LANES), lambda i: (i, 0, R_NSA // LANES)),
                  const((NSA_BLOCK, LANES, LANES)), const((NSA_BLOCK, LANES)), const((1, LANES)),
                  const((LANES, LANES))],
        out_specs=pl.BlockSpec((1, nbp, LANES), lambda i: (i, 0, 0)),
        out_shape=jax.ShapeDtypeStruct((b, nbp, LANES), F32),
        compiler_params=_cparams(("parallel",)),
        name="nsa_compress_prompt",
    )(rows, cp['phi'], cp['cpos'], cp['kn'], cp['seg'])


def _cmp_attn_kernel(q_ref, kv_ref, bias_ref, forced_ref, oc_ref, imp_ref):
    q = q_ref[0]
    kv = kv_ref[0].astype(BF16)
    bias = bias_ref[...]
    vis = bias > 0.5 * NEG
    s = jnp.where(vis, _dot_nt(q, kv[:, :HEAD_DIM]) + bias, NEG)
    p = jnp.where(vis, jnp.exp(s - jnp.max(s, axis=-1, keepdims=True)), 0.0)
    p = p / jnp.maximum(jnp.sum(p, axis=-1, keepdims=True), 1e-30)
    oc_ref[0] = _dot(p.astype(BF16), kv)[:, HEAD_DIM:]
    imp = jnp.sum(_rows3(p), axis=1)
    forced = forced_ref[...]
    imp_ref[0] = jnp.where(forced == 0.0, imp, forced)


def _cmp_attn(q, kvc, bias, forced, n_tok):
    g, r, _ = q.shape
    nbp = kvc.shape[1]
    rows = n_tok * N_HEADS
    steps = r // rows
    per_group_bias = bias.shape[0] == r
    return pl.pallas_call(
        _cmp_attn_kernel,
        grid=(g, steps),
        in_specs=[pl.BlockSpec((1, rows, HEAD_DIM), lambda gi, i: (gi, i, 0)),
                  pl.BlockSpec((1, nbp, LANES), lambda gi, i: (gi, 0, 0)),
                  pl.BlockSpec((rows, nbp), (lambda gi, i: (i, 0)) if per_group_bias else (lambda gi, i: (0, 0))),
                  pl.BlockSpec((n_tok, nbp), (lambda gi, i: (i, 0)) if per_group_bias else (lambda gi, i: (0, 0)))],
        out_specs=[pl.BlockSpec((1, rows, HEAD_DIM), lambda gi, i: (gi, i, 0)),
                   pl.BlockSpec((1, n_tok, nbp), lambda gi, i: (gi, i, 0))],
        out_shape=[jax.ShapeDtypeStruct((g, r, HEAD_DIM), F32),
                   jax.ShapeDtypeStruct((g, r // N_HEADS, nbp), F32)],
        compiler_params=_cparams(("parallel", "parallel")),
        name="nsa_cmp_attn",
    )(q, kvc, bias, forced)


def _topk_kernel(imp_ref, pen_ref, idx_ref):
    x = imp_ref[...]
    lane = _lane(x.shape)
    sel = jnp.zeros(x.shape, jnp.bool_)
    idx = jnp.zeros(x.shape, I32)
    for k in range(NSA_TOPK):
        mx = jnp.max(x, axis=-1, keepdims=True)
        first = jnp.min(jnp.where(x == mx, lane, x.shape[-1]), axis=-1, keepdims=True)
        chosen = lane == first
        sel = jnp.logical_or(sel, chosen)
        idx = jnp.where(lane == k, first, idx)
        x = jnp.where(chosen, -jnp.inf, x)
    pen_ref[...] = jnp.where(sel, 0.0, NEG)
    idx_ref[...] = idx[:, :LANES]


def _topk(imp, tm=256):
    t, nbp = imp.shape
    tm = min(tm, t)
    pen, idx = pl.pallas_call(
        _topk_kernel,
        grid=(t // tm,),
        in_specs=[pl.BlockSpec((tm, nbp), lambda i: (i, 0))],
        out_specs=[pl.BlockSpec((tm, nbp), lambda i: (i, 0)), pl.BlockSpec((tm, LANES), lambda i: (i, 0))],
        out_shape=[jax.ShapeDtypeStruct((t, nbp), F32), jax.ShapeDtypeStruct((t, LANES), I32)],
        compiler_params=_cparams(("parallel",)),
        name="nsa_topk",
    )(imp)
    return pen, idx[:, :NSA_TOPK]


def _rel_bucket(dist):
    exact = REL_BUCKETS // 2
    n = jnp.maximum(dist, 0)
    log_ratio = jnp.log(jnp.maximum(n, 1).astype(F32) / exact) / math.log(REL_MAX_DIST / exact)
    large = jnp.minimum(exact + (log_ratio * (REL_BUCKETS - exact)).astype(I32), REL_BUCKETS - 1)
    return jnp.where(n < exact, n, large)


def _bias_by_dist(rel_bias):
    return rel_bias[_rel_bucket(jnp.arange(REL_MAX_DIST + 1, dtype=I32))]


def _bias_lookup(tbl, dist, visible):
    b = tbl[jnp.clip(dist, 0, REL_MAX_DIST)]
    b = jnp.where(visible[..., None], b, NEG)
    return jnp.transpose(b, (0, 2, 1)).reshape(dist.shape[0] * N_HEADS, dist.shape[1])


def _forced_code(qpos, nbp):
    bq = (qpos // NSA_BLOCK)[:, None]
    bidx = jnp.arange(nbp, dtype=I32)[None, :]
    forced = (bidx == 0) | (bidx == bq) | (bidx == bq - 1)
    return jnp.where(bidx <= bq, jnp.where(forced, NSA_FORCED_SCORE, 0.0), NEG).astype(F32)


def _nsa_prompt_kernel(qa_ref, slc_ref, win_ref, bt_ref, cf_ref, oc_ref, g_ref, o_ref):
    i = pl.program_id(1)
    qa = qa_ref[0]
    rows = qa.shape[0]
    lane_q = _lane(qa.shape)
    q0 = jnp.where(lane_q < HEAD_DIM, qa, jnp.zeros_like(qa))
    cf = cf_ref[...]
    key_row = lax.broadcasted_iota(I32, (Q_TILE, LANES), 0)
    key_lane = lax.broadcasted_iota(I32, (Q_TILE, LANES), 1)

    def sel_tile(j, carry, bias):
        sl = pl.ds(pl.multiple_of(j * Q_TILE, Q_TILE), Q_TILE)
        kv = slc_ref[0, sl, :]
        block = (j * Q_TILE + key_row) // NSA_BLOCK
        k = jnp.where(key_lane < HEAD_DIM, kv, (key_lane - HEAD_DIM == block).astype(F32)).astype(BF16)
        return _online(_dot_nt(qa, k) + bias, kv.astype(BF16), *carry)

    def win_tile(j, carry, bias):
        sl = pl.ds(pl.multiple_of(j * Q_TILE, Q_TILE), Q_TILE)
        kv = win_ref[0, sl, :].astype(BF16)
        return _online(_dot_nt(q0, kv) + bias, kv, *carry)

    def maybe(pred, fn, carry):
        return lax.cond(pred, fn, lambda c: c, carry)

    init = (jnp.full((rows, 1), NEG, F32), jnp.zeros((rows, 1), F32), jnp.zeros((rows, LANES), F32))
    cs = lax.fori_loop(0, jnp.maximum(i - 1, 0), lambda j, c: sel_tile(j, c, cf), init)
    cs = maybe(i >= 1, lambda c: sel_tile(i - 1, c, bt_ref[1]), cs)
    ms, ls, accs = sel_tile(i, cs, bt_ref[0])
    n_far = NSA_WINDOW // Q_TILE
    tq = lax.broadcasted_iota(I32, (rows, Q_TILE), 0) // N_HEADS
    tk = lax.broadcasted_iota(I32, (rows, Q_TILE), 1)
    edge = jnp.where(tk >= tq, cf, NEG)
    cw = maybe(i >= n_far, lambda c: win_tile(i - n_far, c, edge), init)
    for o in range(n_far - 1, 1, -1):
        cw = maybe(i >= o, lambda c, o=o: win_tile(i - o, c, cf), cw)
    cw = maybe(i >= 1, lambda c: win_tile(i - 1, c, bt_ref[1]), cw)
    mw, lw_, accw = win_tile(i, cw, bt_ref[0])
    g = g_ref[0]
    o_s = accs[:, HEAD_DIM:] / jnp.maximum(ls, 1e-30)
    o_w = accw[:, HEAD_DIM:] / jnp.maximum(lw_, 1e-30)
    o_ref[0] = (g[:, 0:1] * oc_ref[0] + g[:, 1:2] * o_s + g[:, 2:3] * o_w).astype(BF16)


def _nsa_prompt(qa, rows, bias_tiles, bias_far, oc, gates):
    b, r, _ = qa.shape
    s = r // N_HEADS
    rt = Q_TILE * N_HEADS
    return pl.pallas_call(
        _nsa_prompt_kernel,
        grid=(b, s // Q_TILE),
        in_specs=[pl.BlockSpec((1, rt, LANES), lambda bi, i: (bi, i, 0)),
                  pl.BlockSpec((1, s, LANES), lambda bi, i: (bi, 0, (R_NSA + LANES) // LANES)),
                  pl.BlockSpec((1, s, LANES), lambda bi, i: (bi, 0, R_WIN // LANES)),
                  pl.BlockSpec((2, rt, LANES), lambda bi, i: (0, 0, 0)),
                  pl.BlockSpec((rt, 1), lambda bi, i: (0, 0)),
                  pl.BlockSpec((1, rt, HEAD_DIM), lambda bi, i: (bi, i, 0)),
                  pl.BlockSpec((1, rt, 3), lambda bi, i: (bi, i, 0))],
        out_specs=pl.BlockSpec((1, rt, HEAD_DIM), lambda bi, i: (bi, i, 0)),
        out_shape=jax.ShapeDtypeStruct((b, r, HEAD_DIM), BF16),
        compiler_params=_cparams(("parallel", "parallel")),
        name="nsa_prompt",
    )(qa, rows, rows, bias_tiles, bias_far, oc, gates)


def _nsa_prompt_branch(qn, rows, cp, tbl):
    b, s, _ = qn.shape
    n_blocks = s // NSA_BLOCK
    assert s % Q_TILE == 0 and n_blocks <= HEAD_DIM and n_blocks >= NSA_TOPK
    nbp = LANES
    pos = jnp.arange(s, dtype=I32)
    kvc = _compress_prompt(rows, cp, nbp)
    rel = jnp.arange(-nbp, nbp, dtype=I32)
    dist_c = jnp.arange(Q_TILE, dtype=I32)[:, None] - (rel[None, :] * NSA_BLOCK + (NSA_BLOCK - 1))
    pattern = _bias_lookup(tbl, dist_c, dist_c >= 0)
    shift = Q_TILE // NSA_BLOCK
    bias_c = jnp.concatenate([pattern[:, nbp - shift * i:2 * nbp - shift * i] for i in range(s // Q_TILE)], axis=0)
    q_rows = qn.reshape(b, s * N_HEADS, HEAD_DIM)
    oc, imp = _cmp_attn(q_rows, kvc, bias_c, _forced_code(pos, nbp), Q_TILE)
    pen, _ = _topk(imp.reshape(b * s, nbp))
    pen = jnp.broadcast_to(pen[:, None, :HEAD_DIM], (b * s, N_HEADS, HEAD_DIM)).astype(BF16)
    qa = jnp.concatenate([q_rows, pen.reshape(b, s * N_HEADS, HEAD_DIM)], axis=-1)
    t_in = jnp.arange(Q_TILE, dtype=I32)
    d0 = t_in[:, None] - t_in[None, :]
    bias_tiles = jnp.stack([_bias_lookup(tbl, d0, d0 >= 0), _bias_lookup(tbl, d0 + Q_TILE, d0 > -Q_TILE)])
    bias_far = jnp.tile(tbl[REL_MAX_DIST], Q_TILE)[:, None]
    g = rows[:, :, R_G + G_GATE:R_G + G_GATE + 3 * N_HEADS].reshape(b, s, 3, N_HEADS)
    g = jnp.transpose(g, (0, 1, 3, 2)).reshape(b, s * N_HEADS, 3)
    out = _nsa_prompt(qa, rows, bias_tiles, bias_far, oc, g)
    return out.reshape(b, s, N_HEADS * HEAD_DIM)


def _attn_out_kernel(x_ref, fox_ref, nsa_ref, mla_ref, w_ref, o_ref):
    w = N_HEADS * HEAD_DIM
    o_ref[...] = (x_ref[...] + _dot(fox_ref[...], w_ref[0:w, :]) + _dot(nsa_ref[...], w_ref[w:2 * w, :])
                  + _dot(mla_ref[...], w_ref[2 * w:, :]))


def _attn_out(x, fox, nsa, mla, w_out, tm=512):
    t = x.shape[0]
    tm = min(tm, t)
    tok = lambda w: pl.BlockSpec((tm, w), lambda i: (i, 0))
    return pl.pallas_call(
        _attn_out_kernel,
        grid=(t // tm,),
        in_specs=[tok(D_MODEL), tok(fox.shape[1]), tok(nsa.shape[1]), tok(mla.shape[1]),
                  pl.BlockSpec((D_MODEL, D_MODEL), lambda i: (0, 0))],
        out_specs=tok(D_MODEL),
        out_shape=jax.ShapeDtypeStruct((t, D_MODEL), F32),
        compiler_params=_cparams(("parallel",)),
        name="attn_out",
    )(x, fox, nsa, mla, w_out)


FF_TILE = 512


def _ffn_kernel(*refs, period, tiles_per_seq):
    if period is None:
        h_ref, g_ref, wa_ref, wb_ref, cw_ref, wd_ref, o_ref, tail_ref, hn_ref, acc_ref, ext_ref, halo_ref = refs
    else:
        h_ref, g_ref, wa_ref, wb_ref, cw_ref, wd_ref, p1_ref, p2_ref, o_ref, a_ref, hn_ref, acc_ref, ext_ref = refs
    i, j = pl.program_id(0), pl.program_id(1)
    tm = h_ref.shape[0]

    @pl.when(j == 0)
    def _():
        h = h_ref[...]
        hn_ref[...] = (h * lax.rsqrt(jnp.mean(h * h, axis=-1, keepdims=True) + EPS) * g_ref[...]).astype(BF16)
        acc_ref[...] = h

    hn = hn_ref[...]
    a = _dot(hn, wa_ref[...])
    b = _dot(hn, wb_ref[...])
    ext_ref[pl.ds(SUBLANES, tm), :] = a
    if period is None:
        @pl.when(i % tiles_per_seq == 0)
        def _():
            halo_ref[j] = jnp.zeros(halo_ref.shape[1:], F32)
        ext_ref[pl.ds(0, SUBLANES), :] = halo_ref[j]
        halo_ref[j] = a[tm - SUBLANES:, :]
        tail_ref[0] = a[tm - SUBLANES:, :]
        a1 = ext_ref[pl.ds(SUBLANES - 1, tm), :]
        a2 = ext_ref[pl.ds(SUBLANES - 2, tm), :]
    else:
        ext_ref[pl.ds(0, SUBLANES), :] = jnp.zeros((SUBLANES, a.shape[1]), F32)
        a_ref[...] = a
        t_in = lax.broadcasted_iota(I32, a.shape, 0) % period
        a1 = jnp.where(t_in >= 1, ext_ref[pl.ds(SUBLANES - 1, tm), :], p1_ref[...])
        a2 = jnp.where(t_in >= 2, ext_ref[pl.ds(SUBLANES - 2, tm), :], p2_ref[...])
    cw = cw_ref[...]
    c = cw[0:1, :] * a2 + cw[1:2, :] * a1 + cw[2:3, :] * a + cw[3:4, :]
    gated = (c / (1.0 + jnp.exp(-c))) * b
    acc_ref[...] += _dot(gated.astype(BF16), wd_ref[...])

    @pl.when(j == pl.num_programs(1) - 1)
    def _():
        o_ref[...] = acc_ref[...]


def _ffn(h, fw, seq_len=None, prev=None, tm=512):
    t = h.shape[0]
    tm = min(tm, t)
    nj = D_FF // FF_TILE
    tokb = lambda w: pl.BlockSpec((tm, w), lambda i, j: (i, 0))
    ffb = pl.BlockSpec((tm, FF_TILE), lambda i, j: (i, j))
    in_specs = [tokb(D_MODEL), pl.BlockSpec((1, D_MODEL), lambda i, j: (0, 0)),
                pl.BlockSpec((D_MODEL, FF_TILE), lambda i, j: (0, j)),
                pl.BlockSpec((D_MODEL, FF_TILE), lambda i, j: (0, j + nj)),
                pl.BlockSpec((SUBLANES, FF_TILE), lambda i, j: (0, j)),
                pl.BlockSpec((FF_TILE, D_MODEL), lambda i, j: (j, 0))]
    args = [h, fw['norm'], fw['up'], fw['up'], fw['conv'], fw['down']]
    scratch = [pltpu.VMEM((tm, D_MODEL), BF16), pltpu.VMEM((tm, D_MODEL), F32),
               pltpu.VMEM((tm + SUBLANES, FF_TILE), F32)]
    if prev is None:
        tiles_per_seq = seq_len // tm
        period = None
        extra_spec = pl.BlockSpec((1, SUBLANES, FF_TILE), lambda i, j: (i, 0, j))
        extra_shape = jax.ShapeDtypeStruct((t // tm, SUBLANES, D_FF), F32)
        scratch.append(pltpu.VMEM((nj, SUBLANES, FF_TILE), F32))
    else:
        tiles_per_seq = None
        period = seq_len
        in_specs += [ffb, ffb]
        args += list(prev)
        extra_spec = ffb
        extra_shape = jax.ShapeDtypeStruct((t, D_FF), F32)
    return pl.pallas_call(
        functools.partial(_ffn_kernel, period=period, tiles_per_seq=tiles_per_seq),
        grid=(t // tm, nj),
        in_specs=in_specs,
        out_specs=[tokb(D_MODEL), extra_spec],
        out_shape=[jax.ShapeDtypeStruct((t, D_MODEL), F32), extra_shape],
        scratch_shapes=scratch,
        compiler_params=_cparams(("arbitrary", "arbitrary")),
        name="ffn",
    )(*args)


def _ple_kernel(h_ref, g_ref, wg_ref, p_ref, wp_ref, o_ref):
    h = h_ref[...]
    hn = (h * lax.rsqrt(jnp.mean(h * h, axis=-1, keepdims=True) + EPS) * g_ref[...]).astype(BF16)
    gate = 1.0 / (1.0 + jnp.exp(-_dot(hn, wg_ref[...])))
    o_ref[...] = h + gate * _dot(p_ref[...].astype(BF16), wp_ref[...])


def _ple(h, p, fw, tm=512):
    t = h.shape[0]
    tm = min(tm, t)
    tok = lambda w: pl.BlockSpec((tm, w), lambda i: (i, 0))
    const = lambda shape: pl.BlockSpec(shape, lambda i: (0,) * len(shape))
    return pl.pallas_call(
        _ple_kernel,
        grid=(t // tm,),
        in_specs=[tok(D_MODEL), const((1, D_MODEL)), const((D_MODEL, D_MODEL)), tok(p.shape[1]),
                  const((p.shape[1], D_MODEL))],
        out_specs=tok(D_MODEL),
        out_shape=jax.ShapeDtypeStruct((t, D_MODEL), F32),
        compiler_params=_cparams(("parallel",)),
        name="ple",
    )(h, fw['ple_norm'], fw['ple_gate'], p, fw['ple_proj'])


def _channel_params(lw):
    conv = jnp.concatenate([lw['ffn_conv'], lw['ffn_conv_b'][None, :],
                            jnp.zeros((SUBLANES - CONV_W - 1, D_FF), F32)], axis=0)
    return dict(w_out=lw['w_out'].astype(BF16), norm=lw['norm_ffn'][None, :], up=lw['ffn_up'].astype(BF16),
                conv=conv, down=lw['ffn_down'].astype(BF16), ple_norm=lw['ple_norm'][None, :],
                ple_gate=lw['ple_gate'].astype(BF16), ple_proj=lw['ple_proj'].astype(BF16))


PAGE = 128
FOX_CHUNK = 2048
MLA_CHUNK = 1024


def _page_copies(pt_ref, seq, n_pages, src_of_page, dst_of_page, sem, wait):
    def body(p, carry):
        cp = pltpu.make_async_copy(src_of_page(pt_ref[seq, p]), dst_of_page(p), sem)
        if wait:
            cp.wait()
        else:
            cp.start()
        return carry
    lax.fori_loop(0, n_pages, body, 0)


def _paged_pipeline(pt_ref, n_pages, src_of_page, dst_of_slot_page, sem_ref):
    b = pl.program_id(0)
    slot = b % 2

    def copies(seq, to_slot, wait):
        _page_copies(pt_ref, seq, n_pages, src_of_page, functools.partial(dst_of_slot_page, to_slot),
                     sem_ref.at[to_slot], wait)

    @pl.when(b == 0)
    def _():
        copies(0, 0, False)

    @pl.when(b + 1 < pl.num_programs(0))
    def _():
        copies(b + 1, 1 - slot, False)

    copies(b, slot, True)
    return slot


def _lane_window(buf_ref):
    return lambda slot, p: buf_ref.at[slot, :, pl.ds(pl.multiple_of(p * PAGE, PAGE), PAGE)]


def _pages_by_token(cache):
    nd = cache.ndim
    t = jnp.transpose(cache, (0, 1) + tuple(range(3, nd)) + (2,))
    return t.reshape(cache.shape[0], cache.shape[1], -1, cache.shape[2])


def _paged_call(kernel_fn, n_seq, n_prefetch, in_specs, out_specs, out_shape, scratch, name, args):
    return pl.pallas_call(
        kernel_fn,
        grid_spec=pltpu.PrefetchScalarGridSpec(num_scalar_prefetch=n_prefetch, grid=(n_seq,), in_specs=in_specs,
                                               out_specs=out_specs, scratch_shapes=scratch),
        out_shape=out_shape,
        compiler_params=_cparams(("arbitrary",)),
        name=name,
    )(*args)


def _seq_block(shape):
    return pl.BlockSpec((1,) + shape, lambda b, *_: (b,) + (0,) * len(shape))


def _const_block(shape):
    return pl.BlockSpec(shape, lambda b, *_: (0,) * len(shape))


_ANY = pl.BlockSpec(memory_space=pl.ANY)


def _scan_paged_kernel(pt_ref, cache_ref, o_ref, buf_ref, sem_ref, *, layer, n_pages):
    slot = _paged_pipeline(pt_ref, n_pages, lambda page: cache_ref.at[layer, page],
                           lambda to_slot, p: buf_ref.at[to_slot, p], sem_ref)
    x3 = buf_ref[slot]
    x = x3.reshape(n_pages * N_HEADS, PAGE)
    lane = _lane(x.shape)
    y = x
    sh = 1
    while sh < PAGE:
        y = y + jnp.where(lane + sh < PAGE, pltpu.roll(y, PAGE - sh, 1), 0.0)
        sh *= 2
    tot = jnp.broadcast_to(y[:, 0:1], x.shape).reshape(x3.shape)
    later = tot
    sh = 1
    while sh < n_pages:
        later = later + jnp.concatenate([later[sh:], jnp.zeros((sh,) + x3.shape[1:], F32)], axis=0)
        sh *= 2
    c = (x - y).reshape(x3.shape) - (later - tot)
    for p in range(n_pages):
        o_ref[0, :, p * PAGE:(p + 1) * PAGE] = c[p]


def _scan_paged(page_table, logf_t, layer):
    n_seq, n_pages = page_table.shape
    return _paged_call(
        functools.partial(_scan_paged_kernel, layer=layer, n_pages=n_pages), n_seq, 1,
        [_ANY], _seq_block((N_HEADS, n_pages * PAGE)), jax.ShapeDtypeStruct((n_seq, N_HEADS, n_pages * PAGE), F32),
        [pltpu.VMEM((2, n_pages, N_HEADS, PAGE), F32), pltpu.SemaphoreType.DMA((2,))],
        "logf_scan_paged", (page_table, logf_t))


def _new_key_mask(s3, n_new):
    tq = lax.broadcasted_iota(I32, s3.shape, 0)
    tk = lax.broadcasted_iota(I32, s3.shape, 2)
    return (tk <= tq) & (tk < n_new)


def _fox_sample_kernel(pt_ref, q_ref, cq_ref, ctp_ref, ctn_ref, new_ref, cache_ref, o_ref, buf_ref, sem_ref,
                       *, layer, n_pages, n_new):
    slot = _paged_pipeline(pt_ref, n_pages, lambda page: cache_ref.at[layer, page], _lane_window(buf_ref), sem_ref)
    q = q_ref[0]
    rows = q.shape[0]
    cq = cq_ref[0]
    tk = min(FOX_CHUNK, n_pages * PAGE)

    def chunk(c, carry):
        sl = pl.ds(pl.multiple_of(c * tk, tk), tk)
        k_t = buf_ref[slot, 0:LANES, sl].astype(BF16)
        v_t = buf_ref[slot, LANES:2 * LANES, sl].astype(BF16)
        ck = ctp_ref[0, :, sl]
        s = _rows3(_dot(q, k_t)) + (_rows3(jnp.broadcast_to(cq, (rows, tk))) - ck[None])
        return _online(s.reshape(rows, tk), v_t, *carry, v_t=True)

    init = (jnp.full((rows, 1), NEG, F32), jnp.zeros((rows, 1), F32), jnp.zeros((rows, LANES), F32))
    carry = lax.fori_loop(0, n_pages * PAGE // tk, chunk, init)
    kv = new_ref[0]
    s = _rows3(_dot_nt(q, kv[:, :LANES].astype(BF16))) + (_rows3(jnp.broadcast_to(cq, (rows, LANES))) - ctn_ref[0][None])
    s = jnp.where(_new_key_mask(s, n_new), s, NEG)
    m, l, acc = _online(s.reshape(rows, LANES), kv[:, LANES:].astype(BF16), *carry)
    o = acc / jnp.maximum(l, 1e-30)
    head = lax.broadcasted_iota(I32, (rows, HEAD_DIM), 0) % N_HEADS
    o_ref[0] = jnp.where(head < FOX_GROUP, o[:, :HEAD_DIM], o[:, HEAD_DIM:]).astype(BF16)


def _pad_keys(x):
    return jnp.pad(x, ((0, 0), (0, LANES - x.shape[1]), (0, 0)))


def _fox_sample(page_table, qf, rows, c_new, c_past_t, kv_t, layer):
    n_seq, n_pages = page_table.shape
    n_new = qf.shape[1]
    r = n_new * N_HEADS
    out = _paged_call(
        functools.partial(_fox_sample_kernel, layer=layer, n_pages=n_pages, n_new=n_new), n_seq, 1,
        [_seq_block((r, LANES)), _seq_block((r, 1)), _seq_block((N_HEADS, n_pages * PAGE)),
         _seq_block((N_HEADS, LANES)), _seq_block((LANES, 2 * LANES)), _ANY],
        _seq_block((r, HEAD_DIM)), jax.ShapeDtypeStruct((n_seq, r, HEAD_DIM), BF16),
        [pltpu.VMEM((2, 2 * LANES, n_pages * PAGE), F32), pltpu.SemaphoreType.DMA((2,))],
        "fox_sample",
        (page_table, qf.reshape(n_seq, r, LANES), c_new.reshape(n_seq, r, 1), c_past_t,
         jnp.transpose(_pad_keys(c_new), (0, 2, 1)), _pad_keys(rows[:, :, R_FOX:R_FOX + 2 * LANES]), kv_t))
    return out.reshape(n_seq, n_new, N_HEADS * HEAD_DIM)


def _mla_sample_kernel(pt_ref, q_ref, lat_ref, g_ref, wuk_ref, wuv_ref, cache_ref, o_ref, buf_ref, sem_ref,
                       stage_ref, *, layer, n_pages, n_new):
    slot = _paged_pipeline(pt_ref, n_pages, lambda page: cache_ref.at[layer, page], _lane_window(buf_ref), sem_ref)
    q = q_ref[0]
    rows = q.shape[0]
    qa, qr = q[:, :LANES], q[:, LANES:]
    wuk = wuk_ref[...]
    scale = MLA_QK_DIM ** -0.5
    tk = min(MLA_CHUNK, n_pages * PAGE)

    def key_norm(kt):
        ssq = jnp.sum((kt * kt).reshape(N_HEADS, MLA_NOPE_DIM, kt.shape[1]), axis=1)
        return lax.rsqrt(ssq * (1.0 / MLA_NOPE_DIM) + EPS)

    def chunk(c, carry):
        sl = pl.ds(pl.multiple_of(c * tk, tk), tk)
        lat_t = buf_ref[slot, 0:MLA_KV_RANK, sl].astype(BF16)
        kr_t = buf_ref[slot, MLA_KV_RANK:MLA_KV_RANK + MLA_ROPE_DIM, sl].astype(BF16)
        s = (_rows3(_dot(qa, lat_t)) * key_norm(_dot(wuk, lat_t))[None]
             + _rows3(_dot(qr[:, :MLA_ROPE_DIM], kr_t))) * scale
        return _online(s.reshape(rows, tk), lat_t, *carry, v_t=True)

    init = (jnp.full((rows, 1), NEG, F32), jnp.zeros((rows, 1), F32), jnp.zeros((rows, LANES), F32))
    carry = lax.fori_loop(0, n_pages * PAGE // tk, chunk, init)
    lat = lat_ref[0].astype(BF16)
    s = (_rows3(_dot_nt(qa, lat)) * key_norm(_dot_nt(wuk, lat))[None]
         + _rows3(_dot_nt(qr, g_ref[0].astype(BF16)))) * scale
    s = jnp.where(_new_key_mask(s, n_new), s, NEG)
    m, l, acc = _online(s.reshape(rows, LANES), lat, *carry)
    _mla_finish(acc, l, wuv_ref, o_ref, stage_ref)


def _mla_sample(page_table, qm, rows, wuk_t, wuv, cache_t, layer):
    n_seq, n_pages = page_table.shape
    n_new = qm.shape[1]
    r = n_new * N_HEADS
    width = MLA_KV_RANK + MLA_ROPE_DIM
    return _paged_call(
        functools.partial(_mla_sample_kernel, layer=layer, n_pages=n_pages, n_new=n_new), n_seq, 1,
        [_seq_block((r, 2 * LANES)), _seq_block((LANES, LANES)), _seq_block((LANES, LANES)),
         _const_block((N_HEADS * MLA_NOPE_DIM, MLA_KV_RANK)), _const_block((N_HEADS, MLA_KV_RANK, MLA_V_DIM)), _ANY],
        _seq_block((n_new, N_HEADS * MLA_V_DIM)), jax.ShapeDtypeStruct((n_seq, n_new, N_HEADS * MLA_V_DIM), BF16),
        [pltpu.VMEM((2, width, n_pages * PAGE), F32), pltpu.SemaphoreType.DMA((2,)), pltpu.VMEM((r, LANES), F32)],
        "mla_sample",
        (page_table, qm.reshape(n_seq, r, 2 * LANES), _pad_keys(rows[:, :, R_LAT:R_LAT + LANES]),
         _pad_keys(rows[:, :, R_G:R_G + LANES]), wuk_t, wuv, cache_t))


def _compress_sample_kernel(pt_ref, phi_ref, cpos_ref, kn_ref, seg_ref, cache_ref, o_ref, buf_ref, sem_ref,
                            *, n_pages):
    slot = _paged_pipeline(
        pt_ref, n_pages, lambda page: cache_ref.at[page],
        lambda to_slot, p: buf_ref.at[to_slot, pl.ds(pl.multiple_of(p * PAGE, PAGE), PAGE)], sem_ref)
    n_blocks = n_pages * PAGE // NSA_BLOCK
    o_ref[0] = jnp.zeros(o_ref.shape[1:], F32)
    o_ref[0, 0:n_blocks, :] = _compress_body(buf_ref.at[slot], phi_ref, cpos_ref, kn_ref, seg_ref, n_blocks)


def _compress_sample(page_table, cp, cmp_pages, nbp):
    n_seq, n_pages = page_table.shape
    return _paged_call(
        functools.partial(_compress_sample_kernel, n_pages=n_pages), n_seq, 1,
        [_const_block((NSA_BLOCK, LANES, LANES)), _const_block((NSA_BLOCK, LANES)), _const_block((1, LANES)),
         _const_block((LANES, LANES)), _ANY],
        _seq_block((nbp, LANES)), jax.ShapeDtypeStruct((n_seq, nbp, LANES), F32),
        [pltpu.VMEM((2, n_pages * PAGE, LANES), F32), pltpu.SemaphoreType.DMA((2,))],
        "nsa_compress_sample", (page_table, cp['phi'], cp['cpos'], cp['kn'], cp['seg'], cmp_pages))


def _nsa_sample_kernel(pt_ref, idx_ref, q_ref, slc_ref, win_ref, state_ref, near_ref, bnew_ref, bwin_ref, cf_ref,
                       oc_ref, g_ref, cache_ref, o_ref, buf_ref, sem_ref, *, layer, n_new, n_blocks):
    b = pl.program_id(0)
    slot = b % 2
    blocks_per_page = PAGE // NSA_BLOCK
    seg = NSA_TOPK * PAGE

    def copies(seq, to_slot, wait):
        def body(e, carry):
            blk = idx_ref[seq, e]
            blk = jnp.where(blk == n_blocks - 1, 0, blk)
            page = pt_ref[seq, blk // blocks_per_page]
            cp = pltpu.make_async_copy(
                cache_ref.at[layer, page, pl.ds(LANES, LANES), :],
                buf_ref.at[to_slot, :, pl.ds(pl.multiple_of(e * PAGE, PAGE), PAGE)],
                sem_ref.at[to_slot])
            if wait:
                cp.wait()
            else:
                cp.start()
            return carry
        lax.fori_loop(0, n_new * NSA_TOPK, body, 0)

    @pl.when(b == 0)
    def _():
        copies(0, 0, False)

    @pl.when(b + 1 < pl.num_programs(0))
    def _():
        copies(b + 1, 1 - slot, False)

    copies(b, slot, True)

    q = q_ref[0]
    cf = cf_ref[...]
    lane = _lane((N_HEADS, LANES))
    new_kv = slc_ref[0]
    bnew = bnew_ref[...]
    vts, s_rows = [], []
    for t in range(n_new):
        r0 = t * N_HEADS
        k_t = buf_ref[slot, 0:HEAD_DIM, pl.ds(t * seg, seg)].astype(BF16)
        vts.append(buf_ref[slot, HEAD_DIM:2 * HEAD_DIM, pl.ds(t * seg, seg)].astype(BF16))
        pieces = []
        for e in range(NSA_TOPK):
            blk = idx_ref[b, t * NSA_TOPK + e]
            bias = jnp.where(blk == n_blocks - 2, near_ref[1, r0:r0 + N_HEADS, :],
                             jnp.where(blk == n_blocks - 3, near_ref[0, r0:r0 + N_HEADS, :],
                                       jnp.broadcast_to(cf[r0:r0 + N_HEADS, :], (N_HEADS, LANES))))
            keep = (lane // NSA_BLOCK == blk % blocks_per_page) & (blk != n_blocks - 1)
            pieces.append(jnp.where(keep, bias, NEG))
        s_all = _dot(q, k_t)
        s_rows.append(s_all[r0:r0 + N_HEADS, :] + jnp.concatenate(pieces, axis=1))
    s1 = jnp.concatenate(s_rows, axis=0)
    new_b = new_kv.astype(BF16)
    s2 = _dot_nt(q, new_b[:, :HEAD_DIM]) + bnew
    m = jnp.maximum(jnp.max(s1, axis=-1, keepdims=True), jnp.max(s2, axis=-1, keepdims=True))
    p1, p2 = jnp.exp(s1 - m), jnp.exp(s2 - m)
    l = jnp.sum(p1, axis=-1, keepdims=True) + jnp.sum(p2, axis=-1, keepdims=True)
    token = lax.broadcasted_iota(I32, p1.shape, 0) // N_HEADS
    o = _dot(p2.astype(BF16), new_b)[:, HEAD_DIM:]
    for t in range(n_new):
        o = o + _dot_nt(jnp.where(token == t, p1, 0.0).astype(BF16), vts[t])
    o_s = o / jnp.maximum(l, 1e-30)
    st = state_ref[0, 0].astype(BF16)
    nw = win_ref[0].astype(BF16)
    s1 = _dot(q, st[:HEAD_DIM, :]) + bwin_ref[...]
    s2 = _dot_nt(q, nw[:, :HEAD_DIM]) + bnew
    m = jnp.maximum(jnp.max(s1, axis=-1, keepdims=True), jnp.max(s2, axis=-1, keepdims=True))
    p1, p2 = jnp.exp(s1 - m), jnp.exp(s2 - m)
    l = jnp.sum(p1, axis=-1, keepdims=True) + jnp.sum(p2, axis=-1, keepdims=True)
    o_w = (_dot_nt(p1.astype(BF16), st[HEAD_DIM:, :]) + _dot(p2.astype(BF16), nw)[:, HEAD_DIM:]) / jnp.maximum(l, 1e-30)
    g = g_ref[0]
    o_ref[0] = (g[:, 0:1] * oc_ref[0] + g[:, 1:2] * o_s + g[:, 2:3] * o_w).astype(BF16)


def _nsa_sample_branch(page_table, qn, rows, cp, tbl, cmp_pages, cache_t, win_state_t, layer):
    n_seq, n_pages = page_table.shape
    n_new = qn.shape[1]
    past = n_pages * PAGE
    n_blocks = past // NSA_BLOCK + 1
    win = win_state_t.shape[3]
    assert n_new <= NSA_BLOCK and n_blocks > NSA_TOPK + 3 and win == NSA_WINDOW
    nbp = -(-n_blocks // LANES) * LANES
    r = n_new * N_HEADS
    qpos = past + jnp.arange(n_new, dtype=I32)
    kvc = _compress_sample(page_table, cp, cmp_pages, nbp)
    cend = jnp.arange(nbp, dtype=I32) * NSA_BLOCK + (NSA_BLOCK - 1)
    dist_c = qpos[:, None] - cend[None, :]
    bias_c = _bias_lookup(tbl, dist_c, dist_c >= 0)
    q_rows = qn.reshape(n_seq, r, HEAD_DIM)
    oc, imp = _cmp_attn(q_rows, kvc, bias_c, _forced_code(qpos, nbp), n_new)
    _, idx = _topk(imp.reshape(n_seq * n_new, nbp))
    idx = idx.reshape(n_seq, n_new * NSA_TOPK)
    u = jnp.arange(LANES, dtype=I32)
    d_near = qpos[:, None] - (past - LANES + u)[None, :]
    near = _bias_lookup(tbl, d_near, d_near >= 0)
    near = jnp.stack([jnp.tile(near[:, :NSA_BLOCK], (1, 2)), jnp.tile(near[:, NSA_BLOCK:], (1, 2))])
    d_new = jnp.arange(n_new, dtype=I32)[:, None] - u[None, :]
    bnew = _bias_lookup(tbl, d_new, (d_new >= 0) & (u < n_new)[None, :])
    d_win = qpos[:, None] - (past - win + jnp.arange(win, dtype=I32))[None, :]
    bwin = _bias_lookup(tbl, d_win, d_win <= NSA_WINDOW)
    cf = jnp.tile(tbl[REL_MAX_DIST], n_new)[:, None]
    g = rows[:, :, R_G + G_GATE:R_G + G_GATE + 3 * N_HEADS].reshape(n_seq, n_new, 3, N_HEADS)
    g = jnp.transpose(g, (0, 1, 3, 2)).reshape(n_seq, r, 3)
    out = _paged_call(
        functools.partial(_nsa_sample_kernel, layer=layer, n_new=n_new, n_blocks=n_blocks), n_seq, 2,
        [_seq_block((r, HEAD_DIM)), _seq_block((LANES, LANES)), _seq_block((LANES, LANES)),
         pl.BlockSpec((1, 1, LANES, win), lambda b, *_: (layer, b, 0, 0)),
         _const_block((2, r, LANES)), _const_block((r, LANES)), _const_block((r, win)), _const_block((r, 1)),
         _seq_block((r, HEAD_DIM)), _seq_block((r, 3)), _ANY],
        _seq_block((r, HEAD_DIM)), jax.ShapeDtypeStruct((n_seq, r, HEAD_DIM), BF16),
        [pltpu.VMEM((2, LANES, n_new * NSA_TOPK * PAGE), F32), pltpu.SemaphoreType.DMA((2,))],
        "nsa_sample",
        (page_table, idx, q_rows, _pad_keys(rows[:, :, R_NSA + LANES:R_NSA + 2 * LANES]),
         _pad_keys(rows[:, :, R_WIN:R_WIN + LANES]), win_state_t, near, bnew, bwin, cf, oc, g, cache_t))
    return out.reshape(n_seq, n_new, N_HEADS * HEAD_DIM)


def _cache_rows(rows):
    n, t, _ = rows.shape
    fox_kv = rows[:, :, R_FOX:R_FOX + 2 * LANES].reshape(n, t, 2, FOX_KV_HEADS, HEAD_DIM)
    logf = rows[:, :, R_G + G_LOGF:R_G + G_LOGF + N_HEADS]
    nsa_kv = rows[:, :, R_NSA:R_NSA + 2 * LANES].reshape(n, t, 4, 1, HEAD_DIM)
    mla = rows[:, :, R_LAT:R_LAT + MLA_KV_RANK + MLA_ROPE_DIM]
    win = rows[:, :, R_WIN:R_WIN + LANES].reshape(n, t, 2, 1, HEAD_DIM)
    return fox_kv, logf, nsa_kv, mla, win


def _channel(x, fox, nsa, mla, p, fw, seq_len, prev=None):
    t = x.shape[0]
    h = _attn_out(x, fox.reshape(t, -1), nsa.reshape(t, -1), mla.reshape(t, -1), fw['w_out'])
    h, conv = _ffn(h, fw, seq_len=seq_len, prev=prev)
    return _ple(h, p.reshape(t, -1), fw), conv


def kernel(x_prompt, x_sample, cache_fox_kv, cache_fox_logf, cache_nsa_kv, cache_mla, state_nsa_win, state_ffn_conv, page_table, p_prompt, p_sample, rel_bias, norm_mix, w_in, fox_qn, fox_kn, fox_fb, nsa_qn, nsa_kn, nsa_phi, nsa_cpos, mla_cqn, mla_wuq, mla_qn, mla_ckvn, mla_wukv, mla_kn, w_out, norm_ffn, ffn_up, ffn_conv, ffn_conv_b, ffn_down, ple_norm, ple_gate, ple_proj):
    layer_params = dict(norm_mix=norm_mix, w_in=w_in, fox_qn=fox_qn, fox_kn=fox_kn, fox_fb=fox_fb, nsa_qn=nsa_qn,
                        nsa_kn=nsa_kn, nsa_phi=nsa_phi, nsa_cpos=nsa_cpos, mla_cqn=mla_cqn, mla_wuq=mla_wuq,
                        mla_qn=mla_qn, mla_ckvn=mla_ckvn, mla_wukv=mla_wukv, mla_kn=mla_kn, w_out=w_out,
                        norm_ffn=norm_ffn, ffn_up=ffn_up, ffn_conv=ffn_conv, ffn_conv_b=ffn_conv_b,
                        ffn_down=ffn_down, ple_norm=ple_norm, ple_gate=ple_gate, ple_proj=ple_proj)
    depth = w_in.shape[0]
    b, s, _ = x_prompt.shape
    n, t_new, _ = x_sample.shape
    n_pool = cache_fox_kv.shape[1]
    assert cache_fox_kv.shape[2] == PAGE
    past = page_table.shape[1] * PAGE
    win_buf = state_nsa_win.shape[2]
    win_keep = min(NSA_WINDOW, s)
    fox_kv_t, logf_t, mla_t, nsa_t = (_pages_by_token(c) for c in (cache_fox_kv, cache_fox_logf, cache_mla,
                                                                   cache_nsa_kv))
    win_state_t = _pages_by_token(state_nsa_win)
    tbl = _bias_by_dist(rel_bias)
    rope_p = jnp.tile(_rope_tables(jnp.arange(s, dtype=I32)), (b, 1))
    rope_s = jnp.tile(_rope_tables(past + jnp.arange(t_new, dtype=I32)), (n, 1))
    hp = x_prompt.reshape(b * s, D_MODEL)
    hs = x_sample.reshape(n * t_new, D_MODEL)
    outs = [[] for _ in range(12)]
    for l in range(depth):
        lw = {k: v[l] for k, v in layer_params.items()}
        pp = _layer_proj_params(lw)
        cp = _compress_params(lw)
        fw = _channel_params(lw)
        qf, qn, qm, rows2 = _project(hp, rope_p, pp)
        rows = rows2.reshape(b, s, ROWS_W)
        fox_kv, logf, nsa_kv, mla_rows, win_rows = _cache_rows(rows)
        fox = _fox_prompt(qf.reshape(b, s, -1), rows, _scan(logf, s))
        mla = _mla_prompt(qm.reshape(b, s, -1), rows, _mla_rs(rows2, pp['wuk_t']).reshape(b, s, N_HEADS), pp['wuv'])
        nsa = _nsa_prompt_branch(qn.reshape(b, s, -1), rows, cp, tbl)
        hp, tail = _channel(hp, fox, nsa, mla, p_prompt[l], fw, s)
        conv_p = tail.reshape(b, -1, SUBLANES, D_FF)[:, -1, SUBLANES - (CONV_W - 1):]
        for o, v in zip(outs[:6], (fox_kv, logf, nsa_kv, mla_rows, win_rows[:, s - win_keep:], conv_p)):
            o.append(v)
        qf, qn, qm, rows2 = _project(hs, rope_s, pp)
        rows = rows2.reshape(n, t_new, ROWS_W)
        fox_kv, logf, nsa_kv, mla_rows, win_rows = _cache_rows(rows)
        c_new = _scan(logf.reshape(1, n * t_new, N_HEADS), t_new).reshape(n, t_new, N_HEADS)
        c_past_t = _scan_paged(page_table, logf_t, l)
        fox = _fox_sample(page_table, qf.reshape(n, t_new, -1), rows, c_new, c_past_t, fox_kv_t, l)
        mla = _mla_sample(page_table, qm.reshape(n, t_new, -1), rows, pp['wuk_t'], pp['wuv'], mla_t, l)
        cmp_pages = cache_nsa_kv[l, :, :, 0:2, 0, :].reshape(n_pool, PAGE, LANES)
        nsa = _nsa_sample_branch(page_table, qn.reshape(n, t_new, -1), rows, cp, tbl, cmp_pages, nsa_t, win_state_t, l)
        st = state_ffn_conv[l]
        zero = jnp.zeros((n, t_new - 1, D_FF), F32)
        prev1 = jnp.concatenate([st[:, 1:2], zero], axis=1).reshape(n * t_new, D_FF)
        prev2 = jnp.concatenate([st[:, 0:2], zero[:, 1:]], axis=1).reshape(n * t_new, D_FF)
        hs, a = _channel(hs, fox, nsa, mla, p_sample[l], fw, t_new, prev=(prev1, prev2))
        win_s = jnp.concatenate([state_nsa_win[l], win_rows], axis=1)[:, -win_buf:]
        conv_s = jnp.concatenate([st, a.reshape(n, t_new, D_FF)], axis=1)[:, -(CONV_W - 1):]
        for o, v in zip(outs[6:], (fox_kv, logf, nsa_kv, mla_rows, win_s, conv_s)):
            o.append(v)
    return (hp.reshape(b, s, D_MODEL), hs.reshape(n, t_new, D_MODEL)) + tuple(jnp.stack(o) for o in outs)
```

```python
import functools
import math

import numpy as np
import jax
import jax.numpy as jnp
from jax import lax
from jax.experimental import pallas as pl
from jax.experimental.pallas import tpu as pltpu

F32, BF16, I32 = jnp.float32, jnp.bfloat16, jnp.int32

D_MODEL = 2048
HEAD_DIM = 64
N_HEADS = 8
FOX_KV_HEADS = 2
FOX_GROUP = N_HEADS // FOX_KV_HEADS
NSA_BLOCK = 64
NSA_TOPK = 16
NSA_WINDOW = 512
NSA_FORCED_SCORE = 1e3
MLA_V_DIM = 128
MLA_Q_RANK = 384
MLA_KV_RANK = 128
MLA_NOPE_DIM = 64
MLA_ROPE_DIM = 32
MLA_QK_DIM = MLA_NOPE_DIM + MLA_ROPE_DIM
ROPE_BASE = 10000.0
REL_BUCKETS = 32
REL_MAX_DIST = 128
D_FF = 5632
CONV_W = 3
EPS = 1e-6
NEG = -1e30

LANES = 128
SUBLANES = 8
VMEM_LIMIT = 56 * 1024 * 1024

Q_TILE = 128
ROWS_W = 7 * LANES
R_FOX, R_NSA, R_WIN, R_LAT, R_G = 0, 256, 512, 640, 768
G_KR, G_LOGF, G_GATE = 0, 32, 40
Z_FQ, Z_NQ, Z_FK, Z_FV, Z_NKV, Z_CKV, Z_CQ, Z_G, Z_END = 0, 1024, 1536, 1664, 1792, 2176, 2304, 2688, 2816


def _cparams(sem, vmem=VMEM_LIMIT):
    return pltpu.CompilerParams(dimension_semantics=sem, vmem_limit_bytes=vmem)


def _dot(a, b):
    return jnp.dot(a, b, preferred_element_type=F32)


def _dot_nt(a, b):
    return lax.dot_general(a, b, (((1,), (1,)), ((), ())), preferred_element_type=F32)


def _seg_mean(x, seg):
    hi = x.astype(BF16)
    lo = (x - hi.astype(F32)).astype(BF16)
    return _dot(hi, seg) + _dot(lo, seg)


def _lane(shape):
    return lax.broadcasted_iota(I32, shape, len(shape) - 1)


def _proj_kernel(x_ref, gmix_ref, w_ref, gain_ref, seg_ref, wuq_ref, gmla_ref, wuk_ref, rope_ref,
                 qf_ref, qn_ref, qm_ref, rows_ref, rowsb_ref):
    x = x_ref[...]
    xn = (x * lax.rsqrt(jnp.mean(x * x, axis=-1, keepdims=True) + EPS) * gmix_ref[...]).astype(BF16)
    z = _dot(xn, w_ref[...])
    gain = gain_ref[...]
    seg_pair, seg_first, seg_kr, seg_mla = seg_ref[0], seg_ref[1], seg_ref[2], seg_ref[3]
    lane = _lane((x.shape[0], LANES))

    def head_norm(lo, seg):
        zs = z[:, lo:lo + LANES]
        return zs * lax.rsqrt(_seg_mean(zs * zs, seg) + EPS) * gain[:, lo:lo + LANES]

    for i in range(8):
        qf_ref[:, i * LANES:(i + 1) * LANES] = head_norm(Z_FQ + i * LANES, seg_pair).astype(BF16)
    for i in range(4):
        qn_ref[:, i * LANES:(i + 1) * LANES] = head_norm(Z_NQ + i * LANES, seg_pair).astype(BF16)
    rows_ref[:, R_FOX:R_FOX + LANES] = head_norm(Z_FK, seg_pair)
    rows_ref[:, R_FOX + LANES:R_FOX + 2 * LANES] = z[:, Z_FV:Z_FV + LANES]
    rows_ref[:, R_NSA:R_NSA + LANES] = z[:, Z_NKV:Z_NKV + LANES]
    for i in (1, 2):
        lo = Z_NKV + i * LANES
        rows_ref[:, R_NSA + i * LANES:R_NSA + (i + 1) * LANES] = jnp.where(
            lane < HEAD_DIM, head_norm(lo, seg_first), z[:, lo:lo + LANES])
    ckv = z[:, Z_CKV:Z_CKV + LANES]
    rows_ref[:, R_LAT:R_LAT + LANES] = (ckv * lax.rsqrt(jnp.mean(ckv * ckv, axis=-1, keepdims=True) + EPS)
                                        * gain[:, Z_CKV:Z_CKV + LANES])
    zg = z[:, Z_G:Z_G + LANES]
    gg = gain[:, Z_G:Z_G + LANES]
    cg, sg = rope_ref[:, 2 * LANES:3 * LANES], rope_ref[:, 3 * LANES:4 * LANES]
    zk = jnp.where(lane < MLA_ROPE_DIM, zg, 0.0)
    kr = zk * lax.rsqrt(_seg_mean(zk * zk, seg_kr) + EPS) * gg
    half = MLA_ROPE_DIM // 2
    kr = kr * cg + jnp.where(lane < half, pltpu.roll(kr, LANES - half, 1), pltpu.roll(kr, half, 1)) * sg
    a = zg + gg
    logf = jnp.minimum(a, 0.0) - jnp.log(1.0 + jnp.exp(-jnp.abs(a)))
    gate = 1.0 / (1.0 + jnp.exp(-zg))
    rows_ref[:, R_G:R_G + LANES] = jnp.where(
        lane < G_LOGF, kr, jnp.where(lane < G_GATE, logf, jnp.where(lane < G_GATE + 3 * N_HEADS, gate, 0.0)))
    cq = z[:, Z_CQ:Z_CQ + MLA_Q_RANK]
    cqn = (cq * lax.rsqrt(jnp.mean(cq * cq, axis=-1, keepdims=True) + EPS)
           * gain[:, Z_CQ:Z_CQ + MLA_Q_RANK]).astype(BF16)
    qh = _dot(cqn, wuq_ref[...])
    cq_t, sq_t = rope_ref[:, 0:LANES], rope_ref[:, LANES:2 * LANES]
    gmla = gmla_ref[...]
    x1_hi = MLA_NOPE_DIM + half
    for h in range(N_HEADS):
        qs = qh[:, h * LANES:(h + 1) * LANES]
        qs = qs * lax.rsqrt(_seg_mean(qs * qs, seg_mla) + EPS) * gmla
        qs = qs * cq_t + jnp.where(lane < x1_hi, pltpu.roll(qs, LANES - half, 1), pltpu.roll(qs, half, 1)) * sq_t
        qa = _dot(qs.astype(BF16), wuk_ref[h])
        qr = jnp.where(lane < MLA_ROPE_DIM, pltpu.roll(qs, LANES - MLA_NOPE_DIM, 1), 0.0)
        qm_ref[:, 2 * h * LANES:(2 * h + 1) * LANES] = qa.astype(BF16)
        qm_ref[:, (2 * h + 1) * LANES:(2 * h + 2) * LANES] = qr.astype(BF16)
    rowsb_ref[...] = rows_ref[...].astype(BF16)


def _seg_matrix(blocks):
    m = np.zeros((LANES, LANES), np.float32)
    for lo, n in blocks:
        m[lo:lo + n, lo:lo + n] = 1.0 / n
    return m


_SEG = np.stack([
    _seg_matrix([(0, 64), (64, 64)]),
    _seg_matrix([(0, 64)]),
    _seg_matrix([(0, MLA_ROPE_DIM)]),
    _seg_matrix([(0, MLA_NOPE_DIM), (MLA_NOPE_DIM, MLA_ROPE_DIM)]),
])


def _layer_proj_params(lw):
    w = lw['w_in']
    o = np.cumsum([0, 512, 128, 128, 8, 512, 384, 24, 384, 128, 32])
    fq, fk, fv, ff, nq, nkv, ng, mcq, mckv, mkr = [w[:, o[i]:o[i + 1]] for i in range(10)]
    zeros = lambda n: jnp.zeros((D_MODEL, n), F32)
    fq_cols = []
    for h in range(N_HEADS):
        part = fq[:, h * HEAD_DIM:(h + 1) * HEAD_DIM]
        fq_cols += [part, zeros(HEAD_DIM)] if h < FOX_GROUP else [zeros(HEAD_DIM), part]
    wp = jnp.concatenate(fq_cols + [nq, fk, fv, nkv, mckv, mcq, mkr, ff, ng, zeros(LANES - 64)], axis=1)
    one = lambda n: jnp.ones((n,), F32)
    zero = lambda n: jnp.zeros((n,), F32)
    scale = HEAD_DIM ** -0.5
    fq_gain = []
    for h in range(N_HEADS):
        fq_gain += [lw['fox_qn'] * scale, zero(HEAD_DIM)] if h < FOX_GROUP else [zero(HEAD_DIM), lw['fox_qn'] * scale]
    gain = jnp.concatenate(fq_gain + [jnp.tile(lw['nsa_qn'] * scale, N_HEADS), jnp.tile(lw['fox_kn'], 2), one(LANES),
                                      one(LANES), lw['nsa_kn'][1], one(64), lw['nsa_kn'][2], one(64),
                                      lw['mla_ckvn'], lw['mla_cqn'], lw['mla_kn'][MLA_NOPE_DIM:], lw['fox_fb'],
                                      zero(LANES - 40)])[None, :]
    wuq = lw['mla_wuq'].reshape(MLA_Q_RANK, N_HEADS, MLA_QK_DIM)
    wuq = jnp.pad(wuq, ((0, 0), (0, 0), (0, LANES - MLA_QK_DIM))).reshape(MLA_Q_RANK, N_HEADS * LANES)
    gmla = jnp.concatenate([lw['mla_qn'], zero(LANES - MLA_QK_DIM)])[None, :]
    wukv = lw['mla_wukv'].reshape(MLA_KV_RANK, N_HEADS, MLA_NOPE_DIM + MLA_V_DIM)
    wuk = wukv[:, :, :MLA_NOPE_DIM]
    wuv = wukv[:, :, MLA_NOPE_DIM:]
    kgain = lw['mla_kn'][:MLA_NOPE_DIM]
    wuk_abs = jnp.transpose(wuk, (1, 2, 0)) * kgain[None, :, None]
    wuk_abs = jnp.pad(wuk_abs, ((0, 0), (0, LANES - MLA_NOPE_DIM), (0, 0)))
    return dict(w=wp.astype(BF16), gain=gain, gmix=lw['norm_mix'][None, :], wuq=wuq.astype(BF16), gmla=gmla,
                wuk_abs=wuk_abs.astype(BF16),
                wuk_t=jnp.transpose(wuk, (1, 2, 0)).reshape(N_HEADS * MLA_NOPE_DIM, MLA_KV_RANK).astype(BF16),
                wuv=jnp.transpose(wuv, (1, 0, 2)).astype(BF16))


def _rope_tables(pos):
    inv = ROPE_BASE ** (-jnp.arange(0, MLA_ROPE_DIM, 2, dtype=F32) / MLA_ROPE_DIM)
    ang = pos.astype(F32)[:, None] * inv
    cos, sin = jnp.cos(ang), jnp.sin(ang)
    n = pos.shape[0]
    one = lambda k: jnp.ones((n, k), F32)
    zero = lambda k: jnp.zeros((n, k), F32)
    half = MLA_ROPE_DIM // 2
    rest = LANES - MLA_NOPE_DIM - MLA_ROPE_DIM
    cq = jnp.concatenate([one(MLA_NOPE_DIM), cos, cos, one(rest)], axis=1)
    sq = jnp.concatenate([zero(MLA_NOPE_DIM), -sin, sin, zero(rest)], axis=1)
    cg = jnp.concatenate([cos, cos, one(LANES - 2 * half)], axis=1)
    sg = jnp.concatenate([-sin, sin, zero(LANES - 2 * half)], axis=1)
    return jnp.concatenate([cq, sq, cg, sg], axis=1)


def _project(x, rope, pp, tm=256):
    t = x.shape[0]
    tm = min(tm, t)
    assert t % tm == 0
    const = lambda shape: pl.BlockSpec(shape, lambda i: (0,) * len(shape))
    tok = lambda w: pl.BlockSpec((tm, w), lambda i: (i, 0))
    return pl.pallas_call(
        _proj_kernel,
        grid=(t // tm,),
        in_specs=[tok(D_MODEL), const((1, D_MODEL)), const((D_MODEL, Z_END)), const((1, Z_END)),
                  const((4, LANES, LANES)), const((MLA_Q_RANK, N_HEADS * LANES)), const((1, LANES)),
                  const((N_HEADS, LANES, LANES)), tok(4 * LANES)],
        out_specs=[tok(N_HEADS * LANES), tok(N_HEADS * HEAD_DIM), tok(2 * N_HEADS * LANES), tok(ROWS_W), tok(ROWS_W)],
        out_shape=[jax.ShapeDtypeStruct((t, N_HEADS * LANES), BF16),
                   jax.ShapeDtypeStruct((t, N_HEADS * HEAD_DIM), BF16),
                   jax.ShapeDtypeStruct((t, 2 * N_HEADS * LANES), BF16),
                   jax.ShapeDtypeStruct((t, ROWS_W), F32),
                   jax.ShapeDtypeStruct((t, ROWS_W), BF16)],
        compiler_params=_cparams(("parallel",)),
        name="proj",
    )(x, pp['gmix'], pp['w'], pp['gain'], jnp.asarray(_SEG, BF16), pp['wuq'], pp['gmla'], pp['wuk_abs'], rope)


POS_PER_ROW = LANES // N_HEADS


def _scan_body(x, buf_ref, seg_len, reverse):
    r = x.shape[0]
    lane = _lane(x.shape)
    pos = lax.broadcasted_iota(I32, x.shape, 0) * POS_PER_ROW + lane // N_HEADS
    segpos = pos % seg_len
    buf_ref[...] = jnp.zeros(buf_ref.shape, F32)

    def row_shift(a, k):
        if k == 0:
            return a
        buf_ref[pl.ds(r, r), :] = a
        return buf_ref[pl.ds(r - k, r), :]

    def shift(y, sh):
        lane_sh, row_sh = (sh % POS_PER_ROW) * N_HEADS, sh // POS_PER_ROW
        sign = -1 if reverse else 1
        if lane_sh == 0:
            return row_shift(y, sign * row_sh)
        a = pltpu.roll(y, (LANES - lane_sh) if reverse else lane_sh, 1)
        same = (lane < LANES - lane_sh) if reverse else (lane >= lane_sh)
        return jnp.where(same, row_shift(a, sign * row_sh), row_shift(a, sign * (row_sh + 1)))

    y = x
    sh = 1
    while sh < seg_len and sh // POS_PER_ROW < r:
        valid = (segpos + sh < seg_len) if reverse else (segpos >= sh)
        y = y + jnp.where(valid, shift(y, sh), 0.0)
        sh *= 2
    return (x - y) if reverse else y


def _scan_kernel(x_ref, o_ref, buf_ref, *, seg_len, reverse):
    o_ref[0] = _scan_body(x_ref[0], buf_ref, seg_len, reverse)


def _scan(x, seg_len, reverse=False):
    n, p_in, h = x.shape
    assert h == N_HEADS
    p = -(-p_in // (POS_PER_ROW * SUBLANES)) * (POS_PER_ROW * SUBLANES)
    x = jnp.pad(x, ((0, 0), (0, p - p_in), (0, 0)))
    r = p // POS_PER_ROW
    out = pl.pallas_call(
        functools.partial(_scan_kernel, seg_len=seg_len, reverse=reverse),
        grid=(n,),
        in_specs=[pl.BlockSpec((1, r, LANES), lambda i: (i, 0, 0))],
        out_specs=pl.BlockSpec((1, r, LANES), lambda i: (i, 0, 0)),
        out_shape=jax.ShapeDtypeStruct((n, r, LANES), F32),
        scratch_shapes=[pltpu.VMEM((3 * r, LANES), F32)],
        compiler_params=_cparams(("parallel",)),
        name="logf_scan",
    )(x.reshape(n, r, LANES))
    return out.reshape(n, p, h)[:, :p_in]


def _online(s, v, m, l, acc, v_t=False):
    m_new = jnp.maximum(m, jnp.max(s, axis=-1, keepdims=True))
    alpha = jnp.exp(m - m_new)
    p = jnp.exp(s - m_new)
    l = alpha * l + jnp.sum(p, axis=-1, keepdims=True)
    pv = _dot_nt(p.astype(BF16), v) if v_t else _dot(p.astype(BF16), v)
    return m_new, l, alpha * acc + pv


def _rows3(x):
    return x.reshape(x.shape[0] // N_HEADS, N_HEADS, x.shape[1])


ROW_CHUNK = 1024
K_TILE = 256
N_CHUNKS = Q_TILE * N_HEADS // ROW_CHUNK


def _softmax_init():
    return (jnp.full((ROW_CHUNK, 1), NEG, F32), jnp.zeros((ROW_CHUNK, 1), F32), jnp.zeros((ROW_CHUNK, LANES), F32))


def _causal_mask(s3, i, r, j):
    tq = i * Q_TILE + r * (ROW_CHUNK // N_HEADS) + lax.broadcasted_iota(I32, s3.shape, 0)
    tk = j * K_TILE + lax.broadcasted_iota(I32, s3.shape, 2)
    return tk <= tq


def _key_tile(j):
    return pl.ds(pl.multiple_of(j * K_TILE, K_TILE), K_TILE)


def _fox_prompt_kernel(q_ref, c_ref, ct_ref, kv_ref, o_ref):
    i = pl.program_id(1)
    n_full = (i * Q_TILE) // K_TILE
    for r in range(N_CHUNKS):
        rs = slice(r * ROW_CHUNK, (r + 1) * ROW_CHUNK)
        q = q_ref[0, rs, :]
        cq = c_ref[0, rs, :]

        def tile(j, carry, diag, q=q, cq=cq, r=r):
            sl = _key_tile(j)
            s = (_rows3(_dot_nt(q, kv_ref[0, sl, 0:LANES]))
                 + (_rows3(jnp.broadcast_to(cq, (ROW_CHUNK, K_TILE))) - ct_ref[0, :, sl][None]))
            if diag:
                s = jnp.where(_causal_mask(s, i, r, j), s, NEG)
            return _online(s.reshape(ROW_CHUNK, K_TILE), kv_ref[0, sl, LANES:2 * LANES], *carry)

        carry = lax.fori_loop(0, n_full, functools.partial(tile, diag=False), _softmax_init())
        m, l, acc = tile(n_full, carry, True)
        o = acc / jnp.maximum(l, 1e-30)
        head = lax.broadcasted_iota(I32, (ROW_CHUNK, HEAD_DIM), 0) % N_HEADS
        o_ref[0, rs, :] = jnp.where(head < FOX_GROUP, o[:, :HEAD_DIM], o[:, HEAD_DIM:]).astype(BF16)


def _fox_prompt(qf, rows, c):
    b, s, _ = qf.shape
    assert s % K_TILE == 0
    nq = s // Q_TILE
    out = pl.pallas_call(
        _fox_prompt_kernel,
        grid=(b, nq),
        in_specs=[pl.BlockSpec((1, Q_TILE * N_HEADS, LANES), lambda bi, i: (bi, i, 0)),
                  pl.BlockSpec((1, Q_TILE * N_HEADS, 1), lambda bi, i: (bi, i, 0)),
                  pl.BlockSpec((1, N_HEADS, s), lambda bi, i: (bi, 0, 0)),
                  pl.BlockSpec((1, s, 2 * LANES), lambda bi, i: (bi, 0, R_FOX // (2 * LANES)))],
        out_specs=pl.BlockSpec((1, Q_TILE * N_HEADS, HEAD_DIM), lambda bi, i: (bi, i, 0)),
        out_shape=jax.ShapeDtypeStruct((b, s * N_HEADS, HEAD_DIM), BF16),
        compiler_params=_cparams(("parallel", "parallel")),
        name="fox_prompt",
    )(qf.reshape(b, s * N_HEADS, LANES), c.reshape(b, s * N_HEADS, 1), jnp.transpose(c, (0, 2, 1)), rows)
    return out.reshape(b, s, N_HEADS * HEAD_DIM)


def _mla_rs_kernel(lat_ref, wuk_ref, red_ref, o_ref):
    k = _dot(lat_ref[...].astype(BF16), wuk_ref[...])
    o_ref[...] = lax.rsqrt(_seg_mean(k * k, red_ref[...]) + EPS)


_HEAD_REDUCE = np.zeros((N_HEADS * MLA_NOPE_DIM, LANES), np.float32)
for _h in range(N_HEADS):
    _HEAD_REDUCE[_h * MLA_NOPE_DIM:(_h + 1) * MLA_NOPE_DIM, _h] = 1.0 / MLA_NOPE_DIM


def _mla_rs(rows2d, wuk_t, tm=512):
    t = rows2d.shape[0]
    tm = min(tm, t)
    out = pl.pallas_call(
        _mla_rs_kernel,
        grid=(t // tm,),
        in_specs=[pl.BlockSpec((tm, LANES), lambda i: (i, R_LAT // LANES)),
                  pl.BlockSpec((MLA_KV_RANK, N_HEADS * MLA_NOPE_DIM), lambda i: (0, 0)),
                  pl.BlockSpec((N_HEADS * MLA_NOPE_DIM, LANES), lambda i: (0, 0))],
        out_specs=pl.BlockSpec((tm, LANES), lambda i: (i, 0)),
        out_shape=jax.ShapeDtypeStruct((t, LANES), F32),
        compiler_params=_cparams(("parallel",)),
        name="mla_rs",
    )(rows2d, wuk_t.T, jnp.asarray(_HEAD_REDUCE, BF16))
    return out[:, :N_HEADS]


def _mla_finish(acc, l, wuv_ref, o_ref, stage_ref):
    stage_ref[...] = acc / jnp.maximum(l, 1e-30)
    _mla_up(wuv_ref, o_ref, stage_ref)


def _mla_up(wuv_ref, o_ref, stage_ref):
    n_tok = stage_ref.shape[0] // N_HEADS
    for h in range(N_HEADS):
        oh = stage_ref[pl.ds(h, n_tok, stride=N_HEADS), :]
        o_ref[0, :, h * MLA_V_DIM:(h + 1) * MLA_V_DIM] = _dot(oh.astype(BF16), wuv_ref[h]).astype(o_ref.dtype)


def _mla_prompt_kernel(q_ref, rst_ref, lat_ref, g_ref, wuv_ref, o_ref, stage_ref):
    i = pl.program_id(1)
    n_full = (i * Q_TILE) // K_TILE
    scale = MLA_QK_DIM ** -0.5
    for r in range(N_CHUNKS):
        rs = slice(r * ROW_CHUNK, (r + 1) * ROW_CHUNK)
        qa, qr = q_ref[0, rs, 0:LANES], q_ref[0, rs, LANES:2 * LANES]

        def tile(j, carry, diag, qa=qa, qr=qr, r=r):
            sl = _key_tile(j)
            lat = lat_ref[0, sl, :]
            s = (_rows3(_dot_nt(qa, lat)) * rst_ref[0, :, sl][None] + _rows3(_dot_nt(qr, g_ref[0, sl, :]))) * scale
            if diag:
                s = jnp.where(_causal_mask(s, i, r, j), s, NEG)
            return _online(s.reshape(ROW_CHUNK, K_TILE), lat, *carry)

        carry = lax.fori_loop(0, n_full, functools.partial(tile, diag=False), _softmax_init())
        m, l, acc = tile(n_full, carry, True)
        stage_ref[rs, :] = acc / jnp.maximum(l, 1e-30)
    _mla_up(wuv_ref, o_ref, stage_ref)


def _mla_prompt(qm, rows, rs, wuv):
    b, s, _ = qm.shape
    assert s % K_TILE == 0
    nq = s // Q_TILE
    return pl.pallas_call(
        _mla_prompt_kernel,
        grid=(b, nq),
        in_specs=[pl.BlockSpec((1, Q_TILE * N_HEADS, 2 * LANES), lambda bi, i: (bi, i, 0)),
                  pl.BlockSpec((1, N_HEADS, s), lambda bi, i: (bi, 0, 0)),
                  pl.BlockSpec((1, s, LANES), lambda bi, i: (bi, 0, R_LAT // LANES)),
                  pl.BlockSpec((1, s, LANES), lambda bi, i: (bi, 0, R_G // LANES)),
                  pl.BlockSpec((N_HEADS, MLA_KV_RANK, MLA_V_DIM), lambda bi, i: (0, 0, 0))],
        out_specs=pl.BlockSpec((1, Q_TILE, N_HEADS * MLA_V_DIM), lambda bi, i: (bi, i, 0)),
        out_shape=jax.ShapeDtypeStruct((b, s, N_HEADS * MLA_V_DIM), BF16),
        scratch_shapes=[pltpu.VMEM((Q_TILE * N_HEADS, LANES), F32)],
        compiler_params=_cparams(("parallel", "parallel")),
        name="mla_prompt",
    )(qm.reshape(b, s * N_HEADS, 2 * LANES), jnp.transpose(rs, (0, 2, 1)), rows, rows, wuv)


def _compress_body(x_ref, phi_ref, cpos_ref, kn_ref, seg_ref, n_blocks):
    accs = [jnp.zeros((n_blocks, LANES), F32) for _ in range(4)]
    for j in range(NSA_BLOCK):
        xj = x_ref[pl.ds(j, n_blocks, stride=NSA_BLOCK), :] + cpos_ref[j:j + 1, :]
        accs[j % 4] = accs[j % 4] + _dot(xj.astype(BF16), phi_ref[j])
    kv = (accs[0] + accs[1]) + (accs[2] + accs[3])
    lane = _lane(kv.shape)
    kc = jnp.where(lane < HEAD_DIM, kv, 0.0)
    kc = kc * lax.rsqrt(_seg_mean(kc * kc, seg_ref[...]) + EPS) * kn_ref[...]
    return jnp.where(lane < HEAD_DIM, kc, kv)


def _compress_params(lw):
    phi = lw['nsa_phi']
    z = jnp.zeros_like(phi[0])
    phi_j = jnp.concatenate([jnp.concatenate([phi[0], z], axis=2), jnp.concatenate([z, phi[1]], axis=2)], axis=1)
    cpos = jnp.concatenate([lw['nsa_cpos'][0], lw['nsa_cpos'][1]], axis=1)
    kn = jnp.concatenate([lw['nsa_kn'][0], jnp.ones((HEAD_DIM,), F32)])[None, :]
    return dict(phi=phi_j.astype(BF16), cpos=cpos, kn=kn, seg=jnp.asarray(_SEG[1], BF16))


def _compress_prompt_kernel(x_ref, phi_ref, cpos_ref, kn_ref, seg_ref, o_ref, *, n_blocks):
    o_ref[0] = jnp.zeros(o_ref.shape[1:], F32)
    o_ref[0, 0:n_blocks, :] = _compress_body(x_ref.at[0], phi_ref, cpos_ref, kn_ref, seg_ref, n_blocks)


def _compress_prompt(rows, cp, nbp):
    b, s, _ = rows.shape
    n_blocks = s // NSA_BLOCK
    const = lambda shape: pl.BlockSpec(shape, lambda i: (0,) * len(shape))
    return pl.pallas_call(
        functools.partial(_compress_prompt_kernel, n_blocks=n_blocks),
        grid=(b,),
        in_specs=[pl.BlockSpec((1, s, LANES), lambda i: (i, 0, R_NSA // LANES)),
                  const((NSA_BLOCK, LANES, LANES)), const((NSA_BLOCK, LANES)), const((1, LANES)),
                  const((LANES, LANES))],
        out_specs=pl.BlockSpec((1, nbp, LANES), lambda i: (i, 0, 0)),
        out_shape=jax.ShapeDtypeStruct((b, nbp, LANES), F32),
        compiler_params=_cparams(("parallel",)),
        name="nsa_compress_prompt",
    )(rows, cp['phi'], cp['cpos'], cp['kn'], cp['seg'])


def _cmp_attn_kernel(q_ref, kv_ref, bias_ref, forced_ref, oc_ref, imp_ref):
    q = q_ref[0]
    kv = kv_ref[0].astype(BF16)
    bias = bias_ref[...]
    vis = bias > 0.5 * NEG
    s = jnp.where(vis, _dot_nt(q, kv[:, :HEAD_DIM]) + bias, NEG)
    p = jnp.where(vis, jnp.exp(s - jnp.max(s, axis=-1, keepdims=True)), 0.0)
    p = p / jnp.maximum(jnp.sum(p, axis=-1, keepdims=True), 1e-30)
    oc_ref[0] = _dot(p.astype(BF16), kv)[:, HEAD_DIM:]
    imp = jnp.sum(_rows3(p), axis=1)
    forced = forced_ref[...]
    imp_ref[0] = jnp.where(forced == 0.0, imp, forced)


def _cmp_attn(q, kvc, bias, forced, n_tok):
    g, r, _ = q.shape
    nbp = kvc.shape[1]
    rows = n_tok * N_HEADS
    steps = r // rows
    per_group_bias = bias.shape[0] == r
    return pl.pallas_call(
        _cmp_attn_kernel,
        grid=(g, steps),
        in_specs=[pl.BlockSpec((1, rows, HEAD_DIM), lambda gi, i: (gi, i, 0)),
                  pl.BlockSpec((1, nbp, LANES), lambda gi, i: (gi, 0, 0)),
                  pl.BlockSpec((rows, nbp), (lambda gi, i: (i, 0)) if per_group_bias else (lambda gi, i: (0, 0))),
                  pl.BlockSpec((n_tok, nbp), (lambda gi, i: (i, 0)) if per_group_bias else (lambda gi, i: (0, 0)))],
        out_specs=[pl.BlockSpec((1, rows, HEAD_DIM), lambda gi, i: (gi, i, 0)),
                   pl.BlockSpec((1, n_tok, nbp), lambda gi, i: (gi, i, 0))],
        out_shape=[jax.ShapeDtypeStruct((g, r, HEAD_DIM), F32),
                   jax.ShapeDtypeStruct((g, r // N_HEADS, nbp), F32)],
        compiler_params=_cparams(("parallel", "parallel")),
        name="nsa_cmp_attn",
    )(q, kvc, bias, forced)


def _topk_kernel(imp_ref, pen_ref, idx_ref):
    x = imp_ref[...]
    lane = _lane(x.shape)
    sel = jnp.zeros(x.shape, jnp.bool_)
    idx = jnp.zeros(x.shape, I32)
    for k in range(NSA_TOPK):
        mx = jnp.max(x, axis=-1, keepdims=True)
        first = jnp.min(jnp.where(x == mx, lane, x.shape[-1]), axis=-1, keepdims=True)
        chosen = lane == first
        sel = jnp.logical_or(sel, chosen)
        idx = jnp.where(lane == k, first, idx)
        x = jnp.where(chosen, -jnp.inf, x)
    pen_ref[...] = jnp.where(sel, 0.0, NEG)
    idx_ref[...] = idx[:, :LANES]


def _topk(imp, tm=256):
    t, nbp = imp.shape
    tm = min(tm, t)
    pen, idx = pl.pallas_call(
        _topk_kernel,
        grid=(t // tm,),
        in_specs=[pl.BlockSpec((tm, nbp), lambda i: (i, 0))],
        out_specs=[pl.BlockSpec((tm, nbp), lambda i: (i, 0)), pl.BlockSpec((tm, LANES), lambda i: (i, 0))],
        out_shape=[jax.ShapeDtypeStruct((t, nbp), F32), jax.ShapeDtypeStruct((t, LANES), I32)],
        compiler_params=_cparams(("parallel",)),
        name="nsa_topk",
    )(imp)
    return pen, idx[:, :NSA_TOPK]


def _rel_bucket(dist):
    exact = REL_BUCKETS // 2
    n = jnp.maximum(dist, 0)
    log_ratio = jnp.log(jnp.maximum(n, 1).astype(F32) / exact) / math.log(REL_MAX_DIST / exact)
    large = jnp.minimum(exact + (log_ratio * (REL_BUCKETS - exact)).astype(I32), REL_BUCKETS - 1)
    return jnp.where(n < exact, n, large)


def _bias_by_dist(rel_bias):
    return rel_bias[_rel_bucket(jnp.arange(REL_MAX_DIST + 1, dtype=I32))]


def _bias_lookup(tbl, dist, visible):
    onehot = jax.nn.one_hot(jnp.clip(dist, 0, REL_MAX_DIST), REL_MAX_DIST + 1, dtype=F32)
    b = jnp.einsum('tkd,dh->tkh', onehot, tbl, precision=lax.Precision.HIGHEST)
    b = jnp.where(visible[..., None], b, NEG)
    return jnp.transpose(b, (0, 2, 1)).reshape(dist.shape[0] * N_HEADS, dist.shape[1])


def _forced_code(qpos, nbp):
    bq = (qpos // NSA_BLOCK)[:, None]
    bidx = jnp.arange(nbp, dtype=I32)[None, :]
    forced = (bidx == 0) | (bidx == bq) | (bidx == bq - 1)
    return jnp.where(bidx <= bq, jnp.where(forced, NSA_FORCED_SCORE, 0.0), NEG).astype(F32)


N_BIAS_TILES = (NSA_WINDOW + Q_TILE) // Q_TILE + 1


def _nsa_prompt_kernel(qa_ref, ka_ref, slc_ref, win_ref, tb_ref, cf_ref, oc_ref, g_ref, o_ref):
    i = pl.program_id(1)
    n_full = (i * Q_TILE) // K_TILE
    odd = (i * Q_TILE % K_TILE) // Q_TILE

    def maybe(pred, fn, carry):
        return lax.cond(pred, fn, lambda c: c, carry)

    for r in range(N_CHUNKS):
        rs = slice(r * ROW_CHUNK, (r + 1) * ROW_CHUNK)
        qa = qa_ref[0, rs, :]
        q0 = jnp.where(_lane(qa.shape) < HEAD_DIM, qa, jnp.zeros_like(qa))
        cf = cf_ref[rs, :]

        def near(k, rs=rs):
            return tb_ref[k, rs, :]

        def sel_tile(j, carry, bias, qa=qa):
            sl = _key_tile(j)
            return _online(_dot_nt(qa, ka_ref[0, sl, :]) + bias, slc_ref[0, sl, :], *carry)

        def win_tile(j, carry, bias, q0=q0):
            kv = win_ref[0, _key_tile(j), :]
            return _online(_dot_nt(q0, kv) + bias, kv, *carry)

        init = _softmax_init()
        cs = lax.fori_loop(0, jnp.maximum(n_full - 1, 0), lambda j, c: sel_tile(j, c, cf), init)
        cs = maybe(n_full >= 1, lambda c: sel_tile(n_full - 1, c, near(2 + odd)), cs)
        ms, ls, accs = sel_tile(n_full, cs, near(odd))
        cw = maybe(n_full >= 2, lambda c: win_tile(n_full - 2, c, near(4 + odd)), init)
        cw = maybe(n_full >= 1, lambda c: win_tile(n_full - 1, c, near(2 + odd)), cw)
        mw, lw_, accw = win_tile(n_full, cw, near(odd))
        g = g_ref[0, rs, :]
        o_s = accs[:, HEAD_DIM:] / jnp.maximum(ls, 1e-30)
        o_w = accw[:, HEAD_DIM:] / jnp.maximum(lw_, 1e-30)
        o_ref[0, rs, :] = (g[:, 0:1] * oc_ref[0, rs, :] + g[:, 1:2] * o_s + g[:, 2:3] * o_w).astype(BF16)


def _nsa_prompt(qa, k_aug, rows, bias_tiles, bias_far, oc, gates):
    b, r, _ = qa.shape
    s = r // N_HEADS
    rt = Q_TILE * N_HEADS
    return pl.pallas_call(
        _nsa_prompt_kernel,
        grid=(b, s // Q_TILE),
        in_specs=[pl.BlockSpec((1, rt, LANES), lambda bi, i: (bi, i, 0)),
                  pl.BlockSpec((1, s, LANES), lambda bi, i: (bi, 0, 0)),
                  pl.BlockSpec((1, s, LANES), lambda bi, i: (bi, 0, (R_NSA + LANES) // LANES)),
                  pl.BlockSpec((1, s, LANES), lambda bi, i: (bi, 0, R_WIN // LANES)),
                  pl.BlockSpec((N_BIAS_TILES, rt, K_TILE), lambda bi, i: (0, 0, 0)),
                  pl.BlockSpec((rt, 1), lambda bi, i: (0, 0)),
                  pl.BlockSpec((1, rt, HEAD_DIM), lambda bi, i: (bi, i, 0)),
                  pl.BlockSpec((1, rt, 3), lambda bi, i: (bi, i, 0))],
        out_specs=pl.BlockSpec((1, rt, HEAD_DIM), lambda bi, i: (bi, i, 0)),
        out_shape=jax.ShapeDtypeStruct((b, r, HEAD_DIM), BF16),
        compiler_params=_cparams(("parallel", "parallel")),
        name="nsa_prompt",
    )(qa, k_aug, rows, rows, bias_tiles, bias_far, oc, gates)


def _nsa_prompt_branch(qn, rows, rows_b, cp, tbl):
    b, s, _ = qn.shape
    n_blocks = s // NSA_BLOCK
    assert s % K_TILE == 0 and n_blocks <= HEAD_DIM and n_blocks >= NSA_TOPK
    nbp = LANES
    pos = jnp.arange(s, dtype=I32)
    kvc = _compress_prompt(rows, cp, nbp)
    rel = jnp.arange(-nbp, nbp, dtype=I32)
    dist_c = jnp.arange(Q_TILE, dtype=I32)[:, None] - (rel[None, :] * NSA_BLOCK + (NSA_BLOCK - 1))
    pattern = _bias_lookup(tbl, dist_c, dist_c >= 0)
    shift = Q_TILE // NSA_BLOCK
    bias_c = jnp.concatenate([pattern[:, nbp - shift * i:2 * nbp - shift * i] for i in range(s // Q_TILE)], axis=0)
    q_rows = qn.reshape(b, s * N_HEADS, HEAD_DIM)
    oc, imp = _cmp_attn(q_rows, kvc, bias_c, _forced_code(pos, nbp), Q_TILE)
    pen, _ = _topk(imp.reshape(b * s, nbp))
    pen = jnp.broadcast_to(pen[:, None, :HEAD_DIM], (b * s, N_HEADS, HEAD_DIM)).astype(BF16)
    qa = jnp.concatenate([q_rows, pen.reshape(b, s * N_HEADS, HEAD_DIM)], axis=-1)
    onehot = (jnp.arange(s, dtype=I32)[:, None] // NSA_BLOCK == jnp.arange(HEAD_DIM, dtype=I32)[None, :])
    k_aug = jnp.concatenate([rows_b[:, :, R_NSA + LANES:R_NSA + LANES + HEAD_DIM],
                             jnp.broadcast_to(onehot.astype(BF16), (b, s, HEAD_DIM))], axis=-1)
    d0 = jnp.arange(Q_TILE, dtype=I32)[:, None] - jnp.arange(K_TILE, dtype=I32)[None, :]
    bias_tiles = jnp.stack([_bias_lookup(tbl, d0 + Q_TILE * k, (d0 + Q_TILE * k >= 0) & (d0 + Q_TILE * k <= NSA_WINDOW))
                            for k in range(N_BIAS_TILES)])
    bias_far = jnp.tile(tbl[REL_MAX_DIST], Q_TILE)[:, None]
    g = rows[:, :, R_G + G_GATE:R_G + G_GATE + 3 * N_HEADS].reshape(b, s, 3, N_HEADS)
    g = jnp.transpose(g, (0, 1, 3, 2)).reshape(b, s * N_HEADS, 3)
    out = _nsa_prompt(qa, k_aug, rows_b, bias_tiles, bias_far, oc, g)
    return out.reshape(b, s, N_HEADS * HEAD_DIM)


def _attn_out_kernel(x_ref, fox_ref, nsa_ref, mla_ref, w_ref, o_ref):
    w = N_HEADS * HEAD_DIM
    o_ref[...] = (x_ref[...] + _dot(fox_ref[...], w_ref[0:w, :]) + _dot(nsa_ref[...], w_ref[w:2 * w, :])
                  + _dot(mla_ref[...], w_ref[2 * w:, :]))


def _attn_out(x, fox, nsa, mla, w_out, tm=512):
    t = x.shape[0]
    tm = min(tm, t)
    tok = lambda w: pl.BlockSpec((tm, w), lambda i: (i, 0))
    return pl.pallas_call(
        _attn_out_kernel,
        grid=(t // tm,),
        in_specs=[tok(D_MODEL), tok(fox.shape[1]), tok(nsa.shape[1]), tok(mla.shape[1]),
                  pl.BlockSpec((D_MODEL, D_MODEL), lambda i: (0, 0))],
        out_specs=tok(D_MODEL),
        out_shape=jax.ShapeDtypeStruct((t, D_MODEL), F32),
        compiler_params=_cparams(("parallel",)),
        name="attn_out",
    )(x, fox, nsa, mla, w_out)


FF_TILE = 512


def _ffn_kernel(*refs, period, tiles_per_seq):
    if period is None:
        h_ref, g_ref, wa_ref, wb_ref, cw_ref, wd_ref, o_ref, tail_ref, hn_ref, acc_ref, ext_ref, halo_ref = refs
    else:
        h_ref, g_ref, wa_ref, wb_ref, cw_ref, wd_ref, p1_ref, p2_ref, o_ref, a_ref, hn_ref, acc_ref, ext_ref = refs
    i, j = pl.program_id(0), pl.program_id(1)
    tm = h_ref.shape[0]

    @pl.when(j == 0)
    def _():
        h = h_ref[...]
        hn_ref[...] = (h * lax.rsqrt(jnp.mean(h * h, axis=-1, keepdims=True) + EPS) * g_ref[...]).astype(BF16)
        acc_ref[...] = h

    hn = hn_ref[...]
    a = _dot(hn, wa_ref[...])
    b = _dot(hn, wb_ref[...])
    ext_ref[pl.ds(SUBLANES, tm), :] = a
    if period is None:
        @pl.when(i % tiles_per_seq == 0)
        def _():
            halo_ref[j] = jnp.zeros(halo_ref.shape[1:], F32)
        ext_ref[pl.ds(0, SUBLANES), :] = halo_ref[j]
        halo_ref[j] = a[tm - SUBLANES:, :]
        tail_ref[0] = a[tm - SUBLANES:, :]
        a1 = ext_ref[pl.ds(SUBLANES - 1, tm), :]
        a2 = ext_ref[pl.ds(SUBLANES - 2, tm), :]
    else:
        ext_ref[pl.ds(0, SUBLANES), :] = jnp.zeros((SUBLANES, a.shape[1]), F32)
        a_ref[...] = a
        t_in = lax.broadcasted_iota(I32, a.shape, 0) % period
        a1 = jnp.where(t_in >= 1, ext_ref[pl.ds(SUBLANES - 1, tm), :], p1_ref[...])
        a2 = jnp.where(t_in >= 2, ext_ref[pl.ds(SUBLANES - 2, tm), :], p2_ref[...])
    cw = cw_ref[...]
    c = cw[0:1, :] * a2 + cw[1:2, :] * a1 + cw[2:3, :] * a + cw[3:4, :]
    gated = (c / (1.0 + jnp.exp(-c))) * b
    acc_ref[...] += _dot(gated.astype(BF16), wd_ref[...])

    @pl.when(j == pl.num_programs(1) - 1)
    def _():
        o_ref[...] = acc_ref[...]


def _ffn(h, fw, seq_len=None, prev=None, tm=512):
    t = h.shape[0]
    tm = min(tm, t)
    nj = D_FF // FF_TILE
    tokb = lambda w: pl.BlockSpec((tm, w), lambda i, j: (i, 0))
    ffb = pl.BlockSpec((tm, FF_TILE), lambda i, j: (i, j))
    in_specs = [tokb(D_MODEL), pl.BlockSpec((1, D_MODEL), lambda i, j: (0, 0)),
                pl.BlockSpec((D_MODEL, FF_TILE), lambda i, j: (0, j)),
                pl.BlockSpec((D_MODEL, FF_TILE), lambda i, j: (0, j + nj)),
                pl.BlockSpec((SUBLANES, FF_TILE), lambda i, j: (0, j)),
                pl.BlockSpec((FF_TILE, D_MODEL), lambda i, j: (j, 0))]
    args = [h, fw['norm'], fw['up'], fw['up'], fw['conv'], fw['down']]
    scratch = [pltpu.VMEM((tm, D_MODEL), BF16), pltpu.VMEM((tm, D_MODEL), F32),
               pltpu.VMEM((tm + SUBLANES, FF_TILE), F32)]
    if prev is None:
        tiles_per_seq = seq_len // tm
        period = None
        extra_spec = pl.BlockSpec((1, SUBLANES, FF_TILE), lambda i, j: (i, 0, j))
        extra_shape = jax.ShapeDtypeStruct((t // tm, SUBLANES, D_FF), F32)
        scratch.append(pltpu.VMEM((nj, SUBLANES, FF_TILE), F32))
    else:
        tiles_per_seq = None
        period = seq_len
        in_specs += [ffb, ffb]
        args += list(prev)
        extra_spec = ffb
        extra_shape = jax.ShapeDtypeStruct((t, D_FF), F32)
    return pl.pallas_call(
        functools.partial(_ffn_kernel, period=period, tiles_per_seq=tiles_per_seq),
        grid=(t // tm, nj),
        in_specs=in_specs,
        out_specs=[tokb(D_MODEL), extra_spec],
        out_shape=[jax.ShapeDtypeStruct((t, D_MODEL), F32), extra_shape],
        scratch_shapes=scratch,
        compiler_params=_cparams(("arbitrary", "arbitrary")),
        name="ffn",
    )(*args)


def _ple_kernel(h_ref, g_ref, wg_ref, p_ref, wp_ref, o_ref):
    h = h_ref[...]
    hn = (h * lax.rsqrt(jnp.mean(h * h, axis=-1, keepdims=True) + EPS) * g_ref[...]).astype(BF16)
    gate = 1.0 / (1.0 + jnp.exp(-_dot(hn, wg_ref[...])))
    o_ref[...] = h + gate * _dot(p_ref[...].astype(BF16), wp_ref[...])


def _ple(h, p, fw, tm=512):
    t = h.shape[0]
    tm = min(tm, t)
    tok = lambda w: pl.BlockSpec((tm, w), lambda i: (i, 0))
    const = lambda shape: pl.BlockSpec(shape, lambda i: (0,) * len(shape))
    return pl.pallas_call(
        _ple_kernel,
        grid=(t // tm,),
        in_specs=[tok(D_MODEL), const((1, D_MODEL)), const((D_MODEL, D_MODEL)), tok(p.shape[1]),
                  const((p.shape[1], D_MODEL))],
        out_specs=tok(D_MODEL),
        out_shape=jax.ShapeDtypeStruct((t, D_MODEL), F32),
        compiler_params=_cparams(("parallel",)),
        name="ple",
    )(h, fw['ple_norm'], fw['ple_gate'], p, fw['ple_proj'])


def _channel_params(lw):
    conv = jnp.concatenate([lw['ffn_conv'], lw['ffn_conv_b'][None, :],
                            jnp.zeros((SUBLANES - CONV_W - 1, D_FF), F32)], axis=0)
    return dict(w_out=lw['w_out'].astype(BF16), norm=lw['norm_ffn'][None, :], up=lw['ffn_up'].astype(BF16),
                conv=conv, down=lw['ffn_down'].astype(BF16), ple_norm=lw['ple_norm'][None, :],
                ple_gate=lw['ple_gate'].astype(BF16), ple_proj=lw['ple_proj'].astype(BF16))


PAGE = 128
FOX_CHUNK = 2048
MLA_CHUNK = 1024


def _page_copies(pt_ref, seq, n_pages, src_of_page, dst_of_page, sem, wait):
    def body(p, carry):
        cp = pltpu.make_async_copy(src_of_page(pt_ref[seq, p]), dst_of_page(p), sem)
        if wait:
            cp.wait()
        else:
            cp.start()
        return carry
    lax.fori_loop(0, n_pages, body, 0, unroll=8)


def _paged_pipeline(pt_ref, n_pages, src_of_page, dst_of_slot_page, sem_ref):
    b = pl.program_id(0)
    slot = b % 2

    def copies(seq, to_slot, wait):
        _page_copies(pt_ref, seq, n_pages, src_of_page, functools.partial(dst_of_slot_page, to_slot),
                     sem_ref.at[to_slot], wait)

    @pl.when(b == 0)
    def _():
        copies(0, 0, False)

    @pl.when(b + 1 < pl.num_programs(0))
    def _():
        copies(b + 1, 1 - slot, False)

    copies(b, slot, True)
    return slot


def _lane_window(buf_ref):
    return lambda slot, p: buf_ref.at[slot, :, pl.ds(pl.multiple_of(p * PAGE, PAGE), PAGE)]


def _pages_by_token(cache):
    nd = cache.ndim
    t = jnp.transpose(cache, (0, 1) + tuple(range(3, nd)) + (2,))
    return t.reshape(cache.shape[0], cache.shape[1], -1, cache.shape[2])


def _paged_call(kernel_fn, n_seq, n_prefetch, in_specs, out_specs, out_shape, scratch, name, args):
    return pl.pallas_call(
        kernel_fn,
        grid_spec=pltpu.PrefetchScalarGridSpec(num_scalar_prefetch=n_prefetch, grid=(n_seq,), in_specs=in_specs,
                                               out_specs=out_specs, scratch_shapes=scratch),
        out_shape=out_shape,
        compiler_params=_cparams(("arbitrary",)),
        name=name,
    )(*args)


def _seq_block(shape):
    return pl.BlockSpec((1,) + shape, lambda b, *_: (b,) + (0,) * len(shape))


def _const_block(shape):
    return pl.BlockSpec(shape, lambda b, *_: (0,) * len(shape))


_ANY = pl.BlockSpec(memory_space=pl.ANY)


def _scan_paged_kernel(pt_ref, cache_ref, o_ref, buf_ref, sem_ref, *, layer, n_pages):
    slot = _paged_pipeline(pt_ref, n_pages, lambda page: cache_ref.at[layer, page],
                           lambda to_slot, p: buf_ref.at[to_slot, p], sem_ref)
    x3 = buf_ref[slot]
    x = x3.reshape(n_pages * N_HEADS, PAGE)
    lane = _lane(x.shape)
    y = x
    sh = 1
    while sh < PAGE:
        y = y + jnp.where(lane + sh < PAGE, pltpu.roll(y, PAGE - sh, 1), 0.0)
        sh *= 2
    tot = jnp.broadcast_to(y[:, 0:1], x.shape).reshape(x3.shape)
    later = tot
    sh = 1
    while sh < n_pages:
        later = later + jnp.concatenate([later[sh:], jnp.zeros((sh,) + x3.shape[1:], F32)], axis=0)
        sh *= 2
    c = (x - y).reshape(x3.shape) - (later - tot)
    for p in range(n_pages):
        o_ref[0, :, p * PAGE:(p + 1) * PAGE] = c[p]


def _scan_paged(page_table, logf_t, layer):
    n_seq, n_pages = page_table.shape
    return _paged_call(
        functools.partial(_scan_paged_kernel, layer=layer, n_pages=n_pages), n_seq, 1,
        [_ANY], _seq_block((N_HEADS, n_pages * PAGE)), jax.ShapeDtypeStruct((n_seq, N_HEADS, n_pages * PAGE), F32),
        [pltpu.VMEM((2, n_pages, N_HEADS, PAGE), F32), pltpu.SemaphoreType.DMA((2,))],
        "logf_scan_paged", (page_table, logf_t))


def _new_key_mask(s3, n_new):
    tq = lax.broadcasted_iota(I32, s3.shape, 0)
    tk = lax.broadcasted_iota(I32, s3.shape, 2)
    return (tk <= tq) & (tk < n_new)


def _fox_sample_kernel(pt_ref, q_ref, cq_ref, ctp_ref, ctn_ref, new_ref, cache_ref, o_ref, buf_ref, sem_ref,
                       *, layer, n_pages, n_new):
    slot = _paged_pipeline(pt_ref, n_pages, lambda page: cache_ref.at[layer, page], _lane_window(buf_ref), sem_ref)
    q = q_ref[0]
    rows = q.shape[0]
    cq = cq_ref[0]
    tk = min(FOX_CHUNK, n_pages * PAGE)

    def chunk(c, carry):
        sl = pl.ds(pl.multiple_of(c * tk, tk), tk)
        k_t = buf_ref[slot, 0:LANES, sl].astype(BF16)
        v_t = buf_ref[slot, LANES:2 * LANES, sl].astype(BF16)
        ck = ctp_ref[0, :, sl]
        s = _rows3(_dot(q, k_t)) + (_rows3(jnp.broadcast_to(cq, (rows, tk))) - ck[None])
        return _online(s.reshape(rows, tk), v_t, *carry, v_t=True)

    init = (jnp.full((rows, 1), NEG, F32), jnp.zeros((rows, 1), F32), jnp.zeros((rows, LANES), F32))
    carry = lax.fori_loop(0, n_pages * PAGE // tk, chunk, init, unroll=2)
    kv = new_ref[0]
    s = _rows3(_dot_nt(q, kv[:, :LANES].astype(BF16))) + (_rows3(jnp.broadcast_to(cq, (rows, LANES))) - ctn_ref[0][None])
    s = jnp.where(_new_key_mask(s, n_new), s, NEG)
    m, l, acc = _online(s.reshape(rows, LANES), kv[:, LANES:].astype(BF16), *carry)
    o = acc / jnp.maximum(l, 1e-30)
    head = lax.broadcasted_iota(I32, (rows, HEAD_DIM), 0) % N_HEADS
    o_ref[0] = jnp.where(head < FOX_GROUP, o[:, :HEAD_DIM], o[:, HEAD_DIM:]).astype(BF16)


def _pad_keys(x):
    return jnp.pad(x, ((0, 0), (0, LANES - x.shape[1]), (0, 0)))


def _fox_sample(page_table, qf, rows, c_new, c_past_t, kv_t, layer):
    n_seq, n_pages = page_table.shape
    n_new = qf.shape[1]
    r = n_new * N_HEADS
    out = _paged_call(
        functools.partial(_fox_sample_kernel, layer=layer, n_pages=n_pages, n_new=n_new), n_seq, 1,
        [_seq_block((r, LANES)), _seq_block((r, 1)), _seq_block((N_HEADS, n_pages * PAGE)),
         _seq_block((N_HEADS, LANES)), _seq_block((LANES, 2 * LANES)), _ANY],
        _seq_block((r, HEAD_DIM)), jax.ShapeDtypeStruct((n_seq, r, HEAD_DIM), BF16),
        [pltpu.VMEM((2, 2 * LANES, n_pages * PAGE), F32), pltpu.SemaphoreType.DMA((2,))],
        "fox_sample",
        (page_table, qf.reshape(n_seq, r, LANES), c_new.reshape(n_seq, r, 1), c_past_t,
         jnp.transpose(_pad_keys(c_new), (0, 2, 1)), _pad_keys(rows[:, :, R_FOX:R_FOX + 2 * LANES]), kv_t))
    return out.reshape(n_seq, n_new, N_HEADS * HEAD_DIM)


def _mla_sample_kernel(pt_ref, q_ref, lat_ref, g_ref, wuk_ref, wuv_ref, cache_ref, o_ref, buf_ref, sem_ref,
                       stage_ref, *, layer, n_pages, n_new):
    slot = _paged_pipeline(pt_ref, n_pages, lambda page: cache_ref.at[layer, page], _lane_window(buf_ref), sem_ref)
    q = q_ref[0]
    rows = q.shape[0]
    qa, qr = q[:, :LANES], q[:, LANES:]
    wuk = wuk_ref[...]
    scale = MLA_QK_DIM ** -0.5
    tk = min(MLA_CHUNK, n_pages * PAGE)

    def key_norm(kt):
        ssq = jnp.sum((kt * kt).reshape(N_HEADS, MLA_NOPE_DIM, kt.shape[1]), axis=1)
        return lax.rsqrt(ssq * (1.0 / MLA_NOPE_DIM) + EPS)

    def chunk(c, carry):
        sl = pl.ds(pl.multiple_of(c * tk, tk), tk)
        lat_t = buf_ref[slot, 0:MLA_KV_RANK, sl].astype(BF16)
        kr_t = buf_ref[slot, MLA_KV_RANK:MLA_KV_RANK + MLA_ROPE_DIM, sl].astype(BF16)
        s = (_rows3(_dot(qa, lat_t)) * key_norm(_dot(wuk, lat_t))[None]
             + _rows3(_dot(qr[:, :MLA_ROPE_DIM], kr_t))) * scale
        return _online(s.reshape(rows, tk), lat_t, *carry, v_t=True)

    init = (jnp.full((rows, 1), NEG, F32), jnp.zeros((rows, 1), F32), jnp.zeros((rows, LANES), F32))
    carry = lax.fori_loop(0, n_pages * PAGE // tk, chunk, init, unroll=2)
    lat = lat_ref[0].astype(BF16)
    s = (_rows3(_dot_nt(qa, lat)) * key_norm(_dot_nt(wuk, lat))[None]
         + _rows3(_dot_nt(qr, g_ref[0].astype(BF16)))) * scale
    s = jnp.where(_new_key_mask(s, n_new), s, NEG)
    m, l, acc = _online(s.reshape(rows, LANES), lat, *carry)
    _mla_finish(acc, l, wuv_ref, o_ref, stage_ref)


def _mla_sample(page_table, qm, rows, wuk_t, wuv, cache_t, layer):
    n_seq, n_pages = page_table.shape
    n_new = qm.shape[1]
    r = n_new * N_HEADS
    width = MLA_KV_RANK + MLA_ROPE_DIM
    return _paged_call(
        functools.partial(_mla_sample_kernel, layer=layer, n_pages=n_pages, n_new=n_new), n_seq, 1,
        [_seq_block((r, 2 * LANES)), _seq_block((LANES, LANES)), _seq_block((LANES, LANES)),
         _const_block((N_HEADS * MLA_NOPE_DIM, MLA_KV_RANK)), _const_block((N_HEADS, MLA_KV_RANK, MLA_V_DIM)), _ANY],
        _seq_block((n_new, N_HEADS * MLA_V_DIM)), jax.ShapeDtypeStruct((n_seq, n_new, N_HEADS * MLA_V_DIM), BF16),
        [pltpu.VMEM((2, width, n_pages * PAGE), F32), pltpu.SemaphoreType.DMA((2,)), pltpu.VMEM((r, LANES), F32)],
        "mla_sample",
        (page_table, qm.reshape(n_seq, r, 2 * LANES), _pad_keys(rows[:, :, R_LAT:R_LAT + LANES]),
         _pad_keys(rows[:, :, R_G:R_G + LANES]), wuk_t, wuv, cache_t))


def _compress_sample_kernel(pt_ref, phi_ref, cpos_ref, kn_ref, seg_ref, cache_ref, o_ref, buf_ref, sem_ref,
                            *, n_pages):
    slot = _paged_pipeline(
        pt_ref, n_pages, lambda page: cache_ref.at[page],
        lambda to_slot, p: buf_ref.at[to_slot, pl.ds(pl.multiple_of(p * PAGE, PAGE), PAGE)], sem_ref)
    n_blocks = n_pages * PAGE // NSA_BLOCK
    o_ref[0] = jnp.zeros(o_ref.shape[1:], F32)
    o_ref[0, 0:n_blocks, :] = _compress_body(buf_ref.at[slot], phi_ref, cpos_ref, kn_ref, seg_ref, n_blocks)


def _compress_sample(page_table, cp, cmp_pages, nbp):
    n_seq, n_pages = page_table.shape
    return _paged_call(
        functools.partial(_compress_sample_kernel, n_pages=n_pages), n_seq, 1,
        [_const_block((NSA_BLOCK, LANES, LANES)), _const_block((NSA_BLOCK, LANES)), _const_block((1, LANES)),
         _const_block((LANES, LANES)), _ANY],
        _seq_block((nbp, LANES)), jax.ShapeDtypeStruct((n_seq, nbp, LANES), F32),
        [pltpu.VMEM((2, n_pages * PAGE, LANES), F32), pltpu.SemaphoreType.DMA((2,))],
        "nsa_compress_sample", (page_table, cp['phi'], cp['cpos'], cp['kn'], cp['seg'], cmp_pages))


def _nsa_sample_kernel(pt_ref, idx_ref, q_ref, slc_ref, win_ref, state_ref, near_ref, bnew_ref, bwin_ref, cf_ref,
                       oc_ref, g_ref, cache_ref, o_ref, buf_ref, sem_ref, *, layer, n_new, n_blocks):
    b = pl.program_id(0)
    slot = b % 2
    blocks_per_page = PAGE // NSA_BLOCK
    seg = NSA_TOPK * PAGE

    def copies(seq, to_slot, wait):
        def body(e, carry):
            blk = idx_ref[seq, e]
            blk = jnp.where(blk == n_blocks - 1, 0, blk)
            page = pt_ref[seq, blk // blocks_per_page]
            cp = pltpu.make_async_copy(
                cache_ref.at[layer, page, pl.ds(LANES, LANES), :],
                buf_ref.at[to_slot, :, pl.ds(pl.multiple_of(e * PAGE, PAGE), PAGE)],
                sem_ref.at[to_slot])
            if wait:
                cp.wait()
            else:
                cp.start()
            return carry
        lax.fori_loop(0, n_new * NSA_TOPK, body, 0)

    @pl.when(b == 0)
    def _():
        copies(0, 0, False)

    @pl.when(b + 1 < pl.num_programs(0))
    def _():
        copies(b + 1, 1 - slot, False)

    copies(b, slot, True)

    q = q_ref[0]
    cf = cf_ref[...]
    lane = _lane((N_HEADS, LANES))
    new_kv = slc_ref[0]
    bnew = bnew_ref[...]
    vts, s_rows = [], []
    for t in range(n_new):
        r0 = t * N_HEADS
        k_t = buf_ref[slot, 0:HEAD_DIM, pl.ds(t * seg, seg)].astype(BF16)
        vts.append(buf_ref[slot, HEAD_DIM:2 * HEAD_DIM, pl.ds(t * seg, seg)].astype(BF16))
        pieces = []
        for e in range(NSA_TOPK):
            blk = idx_ref[b, t * NSA_TOPK + e]
            bias = jnp.where(blk == n_blocks - 2, near_ref[1, r0:r0 + N_HEADS, :],
                             jnp.where(blk == n_blocks - 3, near_ref[0, r0:r0 + N_HEADS, :],
                                       jnp.broadcast_to(cf[r0:r0 + N_HEADS, :], (N_HEADS, LANES))))
            keep = (lane // NSA_BLOCK == blk % blocks_per_page) & (blk != n_blocks - 1)
            pieces.append(jnp.where(keep, bias, NEG))
        s_all = _dot(q, k_t)
        s_rows.append(s_all[r0:r0 + N_HEADS, :] + jnp.concatenate(pieces, axis=1))
    s1 = jnp.concatenate(s_rows, axis=0)
    new_b = new_kv.astype(BF16)
    s2 = _dot_nt(q, new_b[:, :HEAD_DIM]) + bnew
    m = jnp.maximum(jnp.max(s1, axis=-1, keepdims=True), jnp.max(s2, axis=-1, keepdims=True))
    p1, p2 = jnp.exp(s1 - m), jnp.exp(s2 - m)
    l = jnp.sum(p1, axis=-1, keepdims=True) + jnp.sum(p2, axis=-1, keepdims=True)
    token = lax.broadcasted_iota(I32, p1.shape, 0) // N_HEADS
    o = _dot(p2.astype(BF16), new_b)[:, HEAD_DIM:]
    for t in range(n_new):
        o = o + _dot_nt(jnp.where(token == t, p1, 0.0).astype(BF16), vts[t])
    o_s = o / jnp.maximum(l, 1e-30)
    st = state_ref[0, 0].astype(BF16)
    nw = win_ref[0].astype(BF16)
    s1 = _dot(q, st[:HEAD_DIM, :]) + bwin_ref[...]
    s2 = _dot_nt(q, nw[:, :HEAD_DIM]) + bnew
    m = jnp.maximum(jnp.max(s1, axis=-1, keepdims=True), jnp.max(s2, axis=-1, keepdims=True))
    p1, p2 = jnp.exp(s1 - m), jnp.exp(s2 - m)
    l = jnp.sum(p1, axis=-1, keepdims=True) + jnp.sum(p2, axis=-1, keepdims=True)
    o_w = (_dot_nt(p1.astype(BF16), st[HEAD_DIM:, :]) + _dot(p2.astype(BF16), nw)[:, HEAD_DIM:]) / jnp.maximum(l, 1e-30)
    g = g_ref[0]
    o_ref[0] = (g[:, 0:1] * oc_ref[0] + g[:, 1:2] * o_s + g[:, 2:3] * o_w).astype(BF16)


def _nsa_sample_branch(page_table, qn, rows, cp, tbl, cmp_pages, cache_t, win_state_t, layer):
    n_seq, n_pages = page_table.shape
    n_new = qn.shape[1]
    past = n_pages * PAGE
    n_blocks = past // NSA_BLOCK + 1
    win = win_state_t.shape[3]
    assert n_new <= NSA_BLOCK and n_blocks > NSA_TOPK + 3 and win == NSA_WINDOW
    nbp = -(-n_blocks // LANES) * LANES
    r = n_new * N_HEADS
    qpos = past + jnp.arange(n_new, dtype=I32)
    kvc = _compress_sample(page_table, cp, cmp_pages, nbp)
    cend = jnp.arange(nbp, dtype=I32) * NSA_BLOCK + (NSA_BLOCK - 1)
    dist_c = qpos[:, None] - cend[None, :]
    bias_c = _bias_lookup(tbl, dist_c, dist_c >= 0)
    q_rows = qn.reshape(n_seq, r, HEAD_DIM)
    oc, imp = _cmp_attn(q_rows, kvc, bias_c, _forced_code(qpos, nbp), n_new)
    _, idx = _topk(imp.reshape(n_seq * n_new, nbp))
    idx = idx.reshape(n_seq, n_new * NSA_TOPK)
    u = jnp.arange(LANES, dtype=I32)
    d_near = qpos[:, None] - (past - LANES + u)[None, :]
    near = _bias_lookup(tbl, d_near, d_near >= 0)
    near = jnp.stack([jnp.tile(near[:, :NSA_BLOCK], (1, 2)), jnp.tile(near[:, NSA_BLOCK:], (1, 2))])
    d_new = jnp.arange(n_new, dtype=I32)[:, None] - u[None, :]
    bnew = _bias_lookup(tbl, d_new, (d_new >= 0) & (u < n_new)[None, :])
    d_win = qpos[:, None] - (past - win + jnp.arange(win, dtype=I32))[None, :]
    bwin = _bias_lookup(tbl, d_win, d_win <= NSA_WINDOW)
    cf = jnp.tile(tbl[REL_MAX_DIST], n_new)[:, None]
    g = rows[:, :, R_G + G_GATE:R_G + G_GATE + 3 * N_HEADS].reshape(n_seq, n_new, 3, N_HEADS)
    g = jnp.transpose(g, (0, 1, 3, 2)).reshape(n_seq, r, 3)
    out = _paged_call(
        functools.partial(_nsa_sample_kernel, layer=layer, n_new=n_new, n_blocks=n_blocks), n_seq, 2,
        [_seq_block((r, HEAD_DIM)), _seq_block((LANES, LANES)), _seq_block((LANES, LANES)),
         pl.BlockSpec((1, 1, LANES, win), lambda b, *_: (layer, b, 0, 0)),
         _const_block((2, r, LANES)), _const_block((r, LANES)), _const_block((r, win)), _const_block((r, 1)),
         _seq_block((r, HEAD_DIM)), _seq_block((r, 3)), _ANY],
        _seq_block((r, HEAD_DIM)), jax.ShapeDtypeStruct((n_seq, r, HEAD_DIM), BF16),
        [pltpu.VMEM((2, LANES, n_new * NSA_TOPK * PAGE), F32), pltpu.SemaphoreType.DMA((2,))],
        "nsa_sample",
        (page_table, idx, q_rows, _pad_keys(rows[:, :, R_NSA + LANES:R_NSA + 2 * LANES]),
         _pad_keys(rows[:, :, R_WIN:R_WIN + LANES]), win_state_t, near, bnew, bwin, cf, oc, g, cache_t))
    return out.reshape(n_seq, n_new, N_HEADS * HEAD_DIM)


def _cache_rows(rows):
    n, t, _ = rows.shape
    fox_kv = rows[:, :, R_FOX:R_FOX + 2 * LANES].reshape(n, t, 2, FOX_KV_HEADS, HEAD_DIM)
    logf = rows[:, :, R_G + G_LOGF:R_G + G_LOGF + N_HEADS]
    nsa_kv = rows[:, :, R_NSA:R_NSA + 2 * LANES].reshape(n, t, 4, 1, HEAD_DIM)
    mla = rows[:, :, R_LAT:R_LAT + MLA_KV_RANK + MLA_ROPE_DIM]
    win = rows[:, :, R_WIN:R_WIN + LANES].reshape(n, t, 2, 1, HEAD_DIM)
    return fox_kv, logf, nsa_kv, mla, win


def _channel(x, fox, nsa, mla, p, fw, seq_len, prev=None):
    t = x.shape[0]
    h = _attn_out(x, fox.reshape(t, -1), nsa.reshape(t, -1), mla.reshape(t, -1), fw['w_out'])
    h, conv = _ffn(h, fw, seq_len=seq_len, prev=prev)
    return _ple(h, p.reshape(t, -1), fw), conv


def kernel(x_prompt, x_sample, cache_fox_kv, cache_fox_logf, cache_nsa_kv, cache_mla, state_nsa_win, state_ffn_conv, page_table, p_prompt, p_sample, rel_bias, norm_mix, w_in, fox_qn, fox_kn, fox_fb, nsa_qn, nsa_kn, nsa_phi, nsa_cpos, mla_cqn, mla_wuq, mla_qn, mla_ckvn, mla_wukv, mla_kn, w_out, norm_ffn, ffn_up, ffn_conv, ffn_conv_b, ffn_down, ple_norm, ple_gate, ple_proj):
    layer_params = dict(norm_mix=norm_mix, w_in=w_in, fox_qn=fox_qn, fox_kn=fox_kn, fox_fb=fox_fb, nsa_qn=nsa_qn,
                        nsa_kn=nsa_kn, nsa_phi=nsa_phi, nsa_cpos=nsa_cpos, mla_cqn=mla_cqn, mla_wuq=mla_wuq,
                        mla_qn=mla_qn, mla_ckvn=mla_ckvn, mla_wukv=mla_wukv, mla_kn=mla_kn, w_out=w_out,
                        norm_ffn=norm_ffn, ffn_up=ffn_up, ffn_conv=ffn_conv, ffn_conv_b=ffn_conv_b,
                        ffn_down=ffn_down, ple_norm=ple_norm, ple_gate=ple_gate, ple_proj=ple_proj)
    depth = w_in.shape[0]
    b, s, _ = x_prompt.shape
    n, t_new, _ = x_sample.shape
    n_pool = cache_fox_kv.shape[1]
    assert cache_fox_kv.shape[2] == PAGE
    past = page_table.shape[1] * PAGE
    win_buf = state_nsa_win.shape[2]
    win_keep = min(NSA_WINDOW, s)
    fox_kv_t, logf_t, mla_t, nsa_t = (_pages_by_token(c) for c in (cache_fox_kv, cache_fox_logf, cache_mla,
                                                                   cache_nsa_kv))
    win_state_t = _pages_by_token(state_nsa_win)
    tbl = _bias_by_dist(rel_bias)
    rope_p = jnp.tile(_rope_tables(jnp.arange(s, dtype=I32)), (b, 1))
    rope_s = jnp.tile(_rope_tables(past + jnp.arange(t_new, dtype=I32)), (n, 1))
    hp = x_prompt.reshape(b * s, D_MODEL)
    hs = x_sample.reshape(n * t_new, D_MODEL)
    outs = [[] for _ in range(12)]
    for l in range(depth):
        lw = {k: v[l] for k, v in layer_params.items()}
        pp = _layer_proj_params(lw)
        cp = _compress_params(lw)
        fw = _channel_params(lw)
        qf, qn, qm, rows2, rows_b = _project(hp, rope_p, pp)
        rows = rows2.reshape(b, s, ROWS_W)
        rows_b = rows_b.reshape(b, s, ROWS_W)
        fox_kv, logf, nsa_kv, mla_rows, win_rows = _cache_rows(rows)
        fox = _fox_prompt(qf.reshape(b, s, -1), rows_b, _scan(logf, s))
        mla = _mla_prompt(qm.reshape(b, s, -1), rows_b, _mla_rs(rows2, pp['wuk_t']).reshape(b, s, N_HEADS), pp['wuv'])
        nsa = _nsa_prompt_branch(qn.reshape(b, s, -1), rows, rows_b, cp, tbl)
        hp, tail = _channel(hp, fox, nsa, mla, p_prompt[l], fw, s)
        conv_p = tail.reshape(b, -1, SUBLANES, D_FF)[:, -1, SUBLANES - (CONV_W - 1):]
        for o, v in zip(outs[:6], (fox_kv, logf, nsa_kv, mla_rows, win_rows[:, s - win_keep:], conv_p)):
            o.append(v)
        qf, qn, qm, rows2, _ = _project(hs, rope_s, pp)
        rows = rows2.reshape(n, t_new, ROWS_W)
        fox_kv, logf, nsa_kv, mla_rows, win_rows = _cache_rows(rows)
        c_new = _scan(logf.reshape(1, n * t_new, N_HEADS), t_new).reshape(n, t_new, N_HEADS)
        c_past_t = _scan_paged(page_table, logf_t, l)
        fox = _fox_sample(page_table, qf.reshape(n, t_new, -1), rows, c_new, c_past_t, fox_kv_t, l)
        mla = _mla_sample(page_table, qm.reshape(n, t_new, -1), rows, pp['wuk_t'], pp['wuv'], mla_t, l)
        cmp_pages = jnp.transpose(nsa_t[l, :, 0:LANES, :], (0, 2, 1))
        nsa = _nsa_sample_branch(page_table, qn.reshape(n, t_new, -1), rows, cp, tbl, cmp_pages, nsa_t, win_state_t, l)
        st = state_ffn_conv[l]
        zero = jnp.zeros((n, t_new - 1, D_FF), F32)
        prev1 = jnp.concatenate([st[:, 1:2], zero], axis=1).reshape(n * t_new, D_FF)
        prev2 = jnp.concatenate([st[:, 0:2], zero[:, 1:]], axis=1).reshape(n * t_new, D_FF)
        hs, a = _channel(hs, fox, nsa, mla, p_sample[l], fw, t_new, prev=(prev1, prev2))
        win_s = jnp.concatenate([state_nsa_win[l], win_rows], axis=1)[:, -win_buf:]
        conv_s = jnp.concatenate([st, a.reshape(n, t_new, D_FF)], axis=1)[:, -(CONV_W - 1):]
        for o, v in zip(outs[6:], (fox_kv, logf, nsa_kv, mla_rows, win_s, conv_s)):
            o.append(v)
    return (hp.reshape(b, s, D_MODEL), hs.reshape(n, t_new, D_MODEL)) + tuple(jnp.stack(o) for o in outs)
```
